```python
import jax, jax.numpy as jnp
from jax import lax
import numpy as np

D_MODEL = 2048
BATCH = 4
SEQ = 2048
DEPTH = 1
DEC_BATCH = 128
DEC_SEQ = 1
PAST_LEN = 16384
PAGE_SIZE = 128

ML_HEADS = 4
ML_WIDTH = D_MODEL
ML_DV = ML_WIDTH // ML_HEADS
ML_DK = ML_DV // 2
ML_QK = ML_HEADS * ML_DK
SSM_WIDTH = D_MODEL
SSM_HEADDIM = 64
SSM_HEADS = SSM_WIDTH // SSM_HEADDIM
SSM_GROUPS = 2
SSM_STATE = 128
CONV_W = 4
XBC_DIM = SSM_WIDTH + 2 * SSM_GROUPS * SSM_STATE
D_FF = 5632
IN_DIM = 2 * ML_QK + ML_WIDTH + 2 * ML_HEADS + ML_WIDTH + SSM_WIDTH + XBC_DIM + SSM_HEADS + 2 * D_MODEL
CHUNK = 128
EPS = 1e-6
NEG_INIT = -1e30

kernel_name = "hybrid_mlstm_ssd_macaron_step"


def _rms(x):
    xf = x.astype(jnp.float32)
    return xf * lax.rsqrt(jnp.mean(xf * xf, axis=-1, keepdims=True) + EPS)


def rmsnorm(x, w):
    return (_rms(x) * w.astype(jnp.float32)).astype(x.dtype)


def swiglu(x, w_gate, w_up, w_down):
    return (jax.nn.silu(x @ w_gate) * (x @ w_up)) @ w_down


def _chunking(L):
    c = CHUNK if L % CHUNK == 0 else L
    return L // c, c


def causal_conv(xbc, conv_state, w, b):
    L = xbc.shape[1]
    xp = jnp.concatenate([conv_state.astype(xbc.dtype), xbc], axis=1)
    y = sum(xp[:, j:j + L] * w[j] for j in range(CONV_W)) + b
    return jax.nn.silu(y), xp[:, -(CONV_W - 1):]


def mlstm_chunked(q, k, v, li, lf, C0, n0, m0):
    Bsz, L, H, _ = q.shape
    DV = v.shape[-1]
    nc, c = _chunking(L)

    def to_chunks(t):
        t = t.reshape((Bsz, nc, c) + t.shape[2:])
        return jnp.moveaxis(jnp.moveaxis(t, 1, 0), 3, 2)

    causal = jnp.tril(jnp.ones((c, c), dtype=bool))

    def step(carry, inp):
        C, n, m = carry
        qc, kc, vc, lic, lfc = inp
        b = jnp.cumsum(lfc, axis=-1)
        dmat = jnp.where(causal, b[..., :, None] - b[..., None, :] + lic[..., None, :], -jnp.inf)
        inter = b + m[..., None]
        m_t = jnp.maximum(inter, dmat.max(-1))
        w_intra = jnp.exp(dmat - m_t[..., None])
        w_inter = jnp.exp(inter - m_t)
        s = jnp.einsum('bhtd,bhsd->bhts', qc, kc) * w_intra
        num = jnp.einsum('bhts,bhsv->bhtv', s, vc) + w_inter[..., None] * jnp.einsum('bhtd,bhdv->bhtv', qc, C)
        den = s.sum(-1) + w_inter * jnp.einsum('bhtd,bhd->bht', qc, n)
        h = num / jnp.maximum(jnp.abs(den), jnp.exp(-m_t))[..., None]
        m_new = m_t[..., -1]
        w_end = jnp.exp(b[..., -1:] - b + lic - m_new[..., None])
        decay = jnp.exp(b[..., -1] + m - m_new)
        kw = kc * w_end[..., None]
        C_new = decay[..., None, None] * C + jnp.einsum('bhsd,bhsv->bhdv', kw, vc)
        n_new = decay[..., None] * n + kw.sum(-2)
        return (C_new, n_new, m_new), h

    (C1, n1, m1), hs = lax.scan(step, (C0, n0, m0), (to_chunks(q), to_chunks(k), to_chunks(v), to_chunks(li), to_chunks(lf)))
    hs = jnp.swapaxes(jnp.moveaxis(hs, 0, 1), 2, 3).reshape(Bsz, L, H, DV)
    return hs, C1, n1, m1


def ssd_chunked(x, dt, A, Bm, Cm, S0):
    Bsz, L, H, P = x.shape
    G, N = Bm.shape[2], Bm.shape[3]
    R = H // G
    nc, c = _chunking(L)
    xg = x.reshape(Bsz, nc, c, G, R, P).transpose(1, 0, 3, 4, 2, 5)
    dtg = dt.reshape(Bsz, nc, c, G, R).transpose(1, 0, 3, 4, 2)
    ag = dtg * A.reshape(G, R)[None, None, :, :, None]
    Bg = Bm.reshape(Bsz, nc, c, G, N).transpose(1, 0, 3, 2, 4)
    Cg = Cm.reshape(Bsz, nc, c, G, N).transpose(1, 0, 3, 2, 4)
    causal = jnp.tril(jnp.ones((c, c), dtype=bool))

    def step(S, inp):
        xc, dtc, ac, Bc, Cc = inp
        b = jnp.cumsum(ac, axis=-1)
        decay = jnp.exp(jnp.where(causal, b[..., :, None] - b[..., None, :], -jnp.inf))
        cb = jnp.einsum('bgtn,bgsn->bgts', Cc, Bc)
        M = cb[:, :, None] * decay * dtc[..., None, :]
        y = jnp.einsum('bgrts,bgrsp->bgrtp', M, xc) + jnp.exp(b)[..., None] * jnp.einsum('bgtn,bgrpn->bgrtp', Cc, S)
        w_end = jnp.exp(b[..., -1:] - b) * dtc
        S_new = jnp.exp(b[..., -1])[..., None, None] * S + jnp.einsum('bgrs,bgrsp,bgsn->bgrpn', w_end, xc, Bc)
        return S_new, y

    S1, ys = lax.scan(step, S0.reshape(Bsz, G, R, P, N), (xg, dtg, ag, Bg, Cg))
    ys = ys.transpose(1, 0, 4, 2, 3, 5).reshape(Bsz, L, H, P)
    return ys, S1.reshape(Bsz, H, P, N)


def token_mixer(u, conv_st, C0, n0, m0, S0, w_in, ml_i_bias, ml_f_bias, ml_head_norm,
                conv_w, conv_b, dt_bias, A_log, D_skip, ssm_norm, w_out):
    Bsz, L, _ = u.shape
    f32 = jnp.float32
    proj = u @ w_in
    sizes = [ML_QK, ML_QK, ML_WIDTH, ML_HEADS, ML_HEADS, ML_WIDTH, SSM_WIDTH, XBC_DIM, SSM_HEADS, 2 * D_MODEL]
    idx = [int(s) for s in np.cumsum(sizes)[:-1]]
    q, k, v, ig, fg, og, z, xbc, dt_raw, gates = jnp.split(proj, idx, axis=-1)
    q = q.astype(f32).reshape(Bsz, L, ML_HEADS, ML_DK) * (ML_DK ** -0.5)
    k = k.astype(f32).reshape(Bsz, L, ML_HEADS, ML_DK)
    v = v.astype(f32).reshape(Bsz, L, ML_HEADS, ML_DV)
    li = (ig + ml_i_bias).astype(f32)
    lf = jax.nn.log_sigmoid((fg + ml_f_bias).astype(f32))
    h_ml, C1, n1, m1 = mlstm_chunked(q, k, v, li, lf, C0.astype(f32), n0.astype(f32), m0.astype(f32))
    h_ml = (_rms(h_ml) * ml_head_norm.astype(f32).reshape(ML_HEADS, ML_DV)).reshape(Bsz, L, ML_WIDTH)
    y_a = jax.nn.sigmoid(og.astype(f32)) * h_ml
    xbc_act, conv_new = causal_conv(xbc, conv_st, conv_w, conv_b)
    xs, Bm, Cm = jnp.split(xbc_act.astype(f32), [SSM_WIDTH, SSM_WIDTH + SSM_GROUPS * SSM_STATE], axis=-1)
    xs = xs.reshape(Bsz, L, SSM_HEADS, SSM_HEADDIM)
    Bm = Bm.reshape(Bsz, L, SSM_GROUPS, SSM_STATE)
    Cm = Cm.reshape(Bsz, L, SSM_GROUPS, SSM_STATE)
    dt = jax.nn.softplus((dt_raw + dt_bias).astype(f32))
    A = -jnp.exp(A_log.astype(f32))
    y_s, S1 = ssd_chunked(xs, dt, A, Bm, Cm, S0.astype(f32))
    y_s = (y_s + D_skip.astype(f32)[:, None] * xs).reshape(Bsz, L, SSM_WIDTH)
    y_s = y_s * jax.nn.silu(z.astype(f32))
    y_b = _rms(y_s.reshape(Bsz, L, SSM_GROUPS, SSM_WIDTH // SSM_GROUPS)).reshape(Bsz, L, SSM_WIDTH) * ssm_norm.astype(f32)
    g = jax.nn.sigmoid(gates.astype(f32))
    merged = g[..., :D_MODEL] * y_a + g[..., D_MODEL:] * y_b
    out = merged.astype(u.dtype) @ w_out
    return out, conv_new, C1, n1, m1, S1


def trunk_layer(x, conv_st, C0, n0, m0, S0,
                ffn1_norm, ffn1_w_gate, ffn1_w_up, ffn1_w_down, mix_norm, w_in, ml_i_bias, ml_f_bias,
                ml_head_norm, conv_w, conv_b, dt_bias, A_log, D_skip, ssm_norm, w_out,
                ffn2_norm, ffn2_w_gate, ffn2_w_up, ffn2_w_down):
    x = x + 0.5 * swiglu(rmsnorm(x, ffn1_norm), ffn1_w_gate, ffn1_w_up, ffn1_w_down)
    mix, conv_new, C1, n1, m1, S1 = token_mixer(rmsnorm(x, mix_norm), conv_st, C0, n0, m0, S0, w_in,
                                                ml_i_bias, ml_f_bias, ml_head_norm, conv_w, conv_b,
                                                dt_bias, A_log, D_skip, ssm_norm, w_out)
    x = x + mix
    x = x + 0.5 * swiglu(rmsnorm(x, ffn2_norm), ffn2_w_gate, ffn2_w_up, ffn2_w_down)
    return x, conv_new, C1, n1, m1, S1


def setup_inputs(seed: int = 0) -> dict:
    key = jax.random.key(seed)
    ks = jax.random.split(key, 32)
    nrm = jax.random.normal
    f32 = jnp.float32
    dt0 = jnp.exp(jax.random.uniform(ks[20], (DEPTH, SSM_HEADS)) * (np.log(0.1) - np.log(0.001)) + np.log(0.001))
    return {
        "x_prompt": nrm(ks[0], (BATCH, SEQ, D_MODEL), f32),
        "x_sample": nrm(ks[1], (DEC_BATCH, DEC_SEQ, D_MODEL), f32),
        "state_conv": nrm(ks[2], (DEPTH, DEC_BATCH, CONV_W - 1, XBC_DIM), f32),
        "state_mlstm_C": 0.5 * nrm(ks[3], (DEPTH, DEC_BATCH, ML_HEADS, ML_DK, ML_DV), f32),
        "state_mlstm_n": 0.5 * nrm(ks[4], (DEPTH, DEC_BATCH, ML_HEADS, ML_DK), f32),
        "state_mlstm_m": nrm(ks[5], (DEPTH, DEC_BATCH, ML_HEADS), f32),
        "state_ssm": 0.5 * nrm(ks[6], (DEPTH, DEC_BATCH, SSM_HEADS, SSM_HEADDIM, SSM_STATE), f32),
        "ffn1_norm": 1.0 + 0.02 * nrm(ks[7], (DEPTH, D_MODEL), f32),
        "ffn1_w_gate": nrm(ks[8], (DEPTH, D_MODEL, D_FF), f32) * D_MODEL ** -0.5,
        "ffn1_w_up": nrm(ks[9], (DEPTH, D_MODEL, D_FF), f32) * D_MODEL ** -0.5,
        "ffn1_w_down": nrm(ks[10], (DEPTH, D_FF, D_MODEL), f32) * D_FF ** -0.5,
        "mix_norm": 1.0 + 0.02 * nrm(ks[11], (DEPTH, D_MODEL), f32),
        "w_in": nrm(ks[12], (DEPTH, D_MODEL, IN_DIM), f32) * D_MODEL ** -0.5,
        "ml_i_bias": 0.1 * nrm(ks[13], (DEPTH, ML_HEADS), f32),
        "ml_f_bias": jax.random.uniform(ks[14], (DEPTH, ML_HEADS), f32, 3.0, 6.0),
        "ml_head_norm": 1.0 + 0.02 * nrm(ks[15], (DEPTH, ML_WIDTH), f32),
        "ssm_conv_w": 0.5 * nrm(ks[16], (DEPTH, CONV_W, XBC_DIM), f32),
        "ssm_conv_b": 0.02 * nrm(ks[17], (DEPTH, XBC_DIM), f32),
        "ssm_dt_bias": (dt0 + jnp.log(-jnp.expm1(-dt0))).astype(f32),
        "ssm_A_log": jnp.log(jax.random.uniform(ks[18], (DEPTH, SSM_HEADS), f32, 1.0, 16.0)),
        "ssm_D": 1.0 + 0.1 * nrm(ks[19], (DEPTH, SSM_HEADS), f32),
        "ssm_norm": 1.0 + 0.02 * nrm(ks[21], (DEPTH, SSM_WIDTH), f32),
        "w_out": nrm(ks[22], (DEPTH, D_MODEL, D_MODEL), f32) * D_MODEL ** -0.5,
        "ffn2_norm": 1.0 + 0.02 * nrm(ks[23], (DEPTH, D_MODEL), f32),
        "ffn2_w_gate": nrm(ks[24], (DEPTH, D_MODEL, D_FF), f32) * D_MODEL ** -0.5,
        "ffn2_w_up": nrm(ks[25], (DEPTH, D_MODEL, D_FF), f32) * D_MODEL ** -0.5,
        "ffn2_w_down": nrm(ks[26], (DEPTH, D_FF, D_MODEL), f32) * D_FF ** -0.5,
        "final_norm": 1.0 + 0.02 * nrm(ks[27], (D_MODEL,), f32),
    }


def reference(x_prompt, x_sample, state_conv, state_mlstm_C, state_mlstm_n, state_mlstm_m, state_ssm,
              ffn1_norm, ffn1_w_gate, ffn1_w_up, ffn1_w_down, mix_norm, w_in, ml_i_bias, ml_f_bias,
              ml_head_norm, ssm_conv_w, ssm_conv_b, ssm_dt_bias, ssm_A_log, ssm_D, ssm_norm, w_out,
              ffn2_norm, ffn2_w_gate, ffn2_w_up, ffn2_w_down, final_norm):
    f32 = jnp.float32
    Bp = x_prompt.shape[0]
    p_st = (jnp.zeros((Bp, CONV_W - 1, XBC_DIM), x_prompt.dtype),
            jnp.zeros((Bp, ML_HEADS, ML_DK, ML_DV), f32),
            jnp.zeros((Bp, ML_HEADS, ML_DK), f32),
            jnp.full((Bp, ML_HEADS), NEG_INIT, f32),
            jnp.zeros((Bp, SSM_HEADS, SSM_HEADDIM, SSM_STATE), f32))
    xp, xs = x_prompt, x_sample
    pc, pC, pn, pm, pS = [], [], [], [], []
    sc, sC, sn, sm, sS = [], [], [], [], []
    for l in range(DEPTH):
        w = (ffn1_norm[l], ffn1_w_gate[l], ffn1_w_up[l], ffn1_w_down[l], mix_norm[l], w_in[l],
             ml_i_bias[l], ml_f_bias[l], ml_head_norm[l], ssm_conv_w[l], ssm_conv_b[l], ssm_dt_bias[l],
             ssm_A_log[l], ssm_D[l], ssm_norm[l], w_out[l], ffn2_norm[l], ffn2_w_gate[l], ffn2_w_up[l],
             ffn2_w_down[l])
        xp, c1, C1, n1, m1, S1 = trunk_layer(xp, *p_st, *w)
        pc.append(c1); pC.append(C1); pn.append(n1); pm.append(m1); pS.append(S1)
        xs, c2, C2, n2, m2, S2 = trunk_layer(xs, state_conv[l], state_mlstm_C[l], state_mlstm_n[l],
                                             state_mlstm_m[l], state_ssm[l], *w)
        sc.append(c2); sC.append(C2); sn.append(n2); sm.append(m2); sS.append(S2)
    y_prompt = rmsnorm(xp, final_norm)
    y_sample = rmsnorm(xs, final_norm)
    return (y_prompt, y_sample,
            jnp.stack(pc), jnp.stack(pC), jnp.stack(pn), jnp.stack(pm), jnp.stack(pS),
            jnp.stack(sc), jnp.stack(sC), jnp.stack(sn), jnp.stack(sm), jnp.stack(sS))
```

```python
import functools

import jax
import jax.numpy as jnp
from jax import lax
from jax.experimental import pallas as pl
from jax.experimental.pallas import tpu as pltpu

F32 = jnp.float32
BF16 = jnp.bfloat16

EPS = 1e-6
NEG_INIT = -1e30

D_MODEL = 2048
ML_HEADS = 4
ML_DV = 512
ML_DK = 256
ML_QK = ML_HEADS * ML_DK
SSM_HEADS = 32
SSM_HEADDIM = 64
SSM_GROUPS = 2
SSM_STATE = 128
SSM_WIDTH = SSM_HEADS * SSM_HEADDIM
CONV_W = 4
BC_DIM = 2 * SSM_GROUPS * SSM_STATE
XBC_DIM = SSM_WIDTH + BC_DIM
CHUNK = 128

LANES = 128
SUBLANES = 8

SEG_GATES = 0
SEG_OG = SEG_GATES + 2 * D_MODEL
SEG_Z = SEG_OG + D_MODEL
SEG_V = SEG_Z + SSM_WIDTH
SEG_XS = SEG_V + D_MODEL
SEG_Q = SEG_XS + SSM_WIDTH
SEG_K = SEG_Q + ML_QK
SEG_BC = SEG_K + ML_QK
SEG_SMALL = SEG_BC + BC_DIM
PROJ_DIM = SEG_SMALL + LANES
SM_DT = 0
SM_IG = SM_DT + SSM_HEADS
SM_FG = SM_IG + ML_HEADS

VMEM_LIMIT = 56 * 1024 * 1024


def _rms(x):
    return x * lax.rsqrt(jnp.mean(x * x, axis=-1, keepdims=True) + EPS)


def _sigmoid(x):
    return 1.0 / (1.0 + jnp.exp(-x))


def _silu(x):
    return x * _sigmoid(x)


def _softplus(x):
    return jnp.maximum(x, 0.0) + jnp.log(1.0 + jnp.exp(-jnp.abs(x)))


def _dot(a, b):
    return jnp.dot(a, b, preferred_element_type=F32)


def _dot_nt(a, b):
    return lax.dot_general(a, b, (((1,), (1,)), ((), ())), preferred_element_type=F32)


def _dot_tn(a, b):
    return lax.dot_general(a, b, (((0,), (0,)), ((), ())), preferred_element_type=F32)


def _split3(x):
    hi = x.astype(BF16)
    r1 = x - hi.astype(F32)
    mid = r1.astype(BF16)
    lo = (r1 - mid.astype(F32)).astype(BF16)
    return hi, mid, lo


def _ffn_kernel(x_ref, nw_ref, wg_ref, wu_ref, wd_ref, fw_ref, o_ref, h_ref, *, final_norm):
    j = pl.program_id(1)
    last = pl.num_programs(1) - 1

    @pl.when(j == 0)
    def _():
        h_ref[...] = (_rms(x_ref[...]) * nw_ref[...]).astype(BF16)
        o_ref[...] = jnp.zeros(o_ref.shape, F32)

    h = h_ref[...]
    g = _dot(h, wg_ref[...].astype(BF16))
    u = _dot(h, wu_ref[...].astype(BF16))
    a = (_silu(g) * u).astype(BF16)
    o_ref[...] += _dot(a, wd_ref[...].astype(BF16))

    @pl.when(j == last)
    def _():
        y = x_ref[...] + 0.5 * o_ref[...]
        if final_norm:
            y = _rms(y) * fw_ref[...]
        o_ref[...] = y


def _ffn(x, norm_w, w_gate, w_up, w_down, final_w, *, final_norm, tm, tf):
    m, d = x.shape
    f = w_gate.shape[1]
    return pl.pallas_call(
        functools.partial(_ffn_kernel, final_norm=final_norm),
        grid=(m // tm, f // tf),
        in_specs=[
            pl.BlockSpec((tm, d), lambda i, j: (i, 0), pipeline_mode=pl.Buffered(1)),
            pl.BlockSpec((1, d), lambda i, j: (0, 0)),
            pl.BlockSpec((d, tf), lambda i, j: (0, j)),
            pl.BlockSpec((d, tf), lambda i, j: (0, j)),
            pl.BlockSpec((tf, d), lambda i, j: (j, 0)),
            pl.BlockSpec((1, d), lambda i, j: (0, 0)),
        ],
        out_specs=pl.BlockSpec((tm, d), lambda i, j: (i, 0)),
        out_shape=jax.ShapeDtypeStruct((m, d), F32),
        scratch_shapes=[pltpu.VMEM((tm, d), BF16)],
        compiler_params=pltpu.CompilerParams(
            dimension_semantics=("arbitrary", "arbitrary"), vmem_limit_bytes=VMEM_LIMIT),
        name="ffn",
    )(x, norm_w.reshape(1, d), w_gate, w_up, w_down, final_w.reshape(1, d))


def _inproj_kernel(x_ref, nw_ref, w_ref, o_ref, u_ref):
    @pl.when(pl.program_id(1) == 0)
    def _():
        u_ref[...] = (_rms(x_ref[...]) * nw_ref[...]).astype(BF16)

    o_ref[...] = _dot(u_ref[...], w_ref[...])


def _inproj(x, norm_w, w, *, tm, tn):
    m, d = x.shape
    n = w.shape[1]
    return pl.pallas_call(
        _inproj_kernel,
        grid=(m // tm, n // tn),
        in_specs=[
            pl.BlockSpec((tm, d), lambda i, j: (i, 0), pipeline_mode=pl.Buffered(1)),
            pl.BlockSpec((1, d), lambda i, j: (0, 0)),
            pl.BlockSpec((d, tn), lambda i, j: (0, j)),
        ],
        out_specs=pl.BlockSpec((tm, tn), lambda i, j: (i, j)),
        out_shape=jax.ShapeDtypeStruct((m, n), F32),
        scratch_shapes=[pltpu.VMEM((tm, d), BF16)],
        compiler_params=pltpu.CompilerParams(
            dimension_semantics=("arbitrary", "arbitrary"), vmem_limit_bytes=VMEM_LIMIT),
        name="inproj",
    )(x, norm_w.reshape(1, d), w)


def _outproj_kernel(y_ref, w_ref, x_ref, o_ref):
    o_ref[...] = x_ref[...] + _dot(y_ref[...].astype(BF16), w_ref[...].astype(BF16))


def _outproj(y, w, x, *, tm, tn):
    m, d = y.shape
    n = w.shape[1]
    return pl.pallas_call(
        _outproj_kernel,
        grid=(m // tm, n // tn),
        in_specs=[
            pl.BlockSpec((tm, d), lambda i, j: (i, 0)),
            pl.BlockSpec((d, tn), lambda i, j: (0, j)),
            pl.BlockSpec((tm, tn), lambda i, j: (i, j)),
        ],
        out_specs=pl.BlockSpec((tm, tn), lambda i, j: (i, j)),
        out_shape=jax.ShapeDtypeStruct((m, n), F32),
        compiler_params=pltpu.CompilerParams(
            dimension_semantics=("arbitrary", "arbitrary"), vmem_limit_bytes=VMEM_LIMIT),
        name="outproj",
    )(y, w, x)


HIST = CONV_W - 1
PAD_TOP = SUBLANES


def _mixer_kernel(gates_ref, og_ref, z_ref, v_ref, xs_ref, q_ref, k_ref, bc_ref, sm_ref,
                  conv0_ref, c0_ref, n0_ref, m0_ref, s0_ref,
                  bias_ref, alog_ref, convw_ref, convb_ref, dskip_ref, hnorm_ref, snorm_ref, e1_ref,
                  merged_ref, conv_ref, c_ref, n_ref, m_ref, s_ref,
                  xsp_ref, bcp_ref, *, single_row):
    b = pl.program_id(0)
    c = pl.program_id(1)
    nv = 1 if single_row else CHUNK
    r = b % SUBLANES

    def load_rows(ref):
        if not single_row:
            return ref[...]
        row = ref[pl.ds(r, 1), :]
        rows = lax.broadcasted_iota(jnp.int32, (CHUNK, row.shape[1]), 0)
        return jnp.where(rows == 0, row, 0.0)

    @pl.when(c == 0)
    def _():
        c_ref[...] = c0_ref[...]
        n_ref[...] = n0_ref[...]
        m_ref[...] = m0_ref[...]
        s_ref[...] = s0_ref[...]
        xsp_ref[0:PAD_TOP, :] = jnp.zeros((PAD_TOP, SSM_WIDTH), F32)
        bcp_ref[0:PAD_TOP, :] = jnp.zeros((PAD_TOP, BC_DIM), F32)
        xsp_ref[PAD_TOP - HIST:PAD_TOP, :] = conv0_ref[0, :, 0:SSM_WIDTH]
        bcp_ref[PAD_TOP - HIST:PAD_TOP, :] = conv0_ref[0, :, SSM_WIDTH:XBC_DIM]

    xsp_ref[PAD_TOP:PAD_TOP + CHUNK, :] = load_rows(xs_ref)
    bcp_ref[PAD_TOP:PAD_TOP + CHUNK, :] = load_rows(bc_ref)

    def conv(p_ref, lo, hi):
        acc = p_ref[PAD_TOP - HIST:PAD_TOP - HIST + CHUNK, :] * convw_ref[0:1, lo:hi]
        for j in range(1, CONV_W):
            acc = acc + p_ref[PAD_TOP - HIST + j:PAD_TOP - HIST + j + CHUNK, :] * convw_ref[j:j + 1, lo:hi]
        return _silu(acc + convb_ref[:, lo:hi])

    xs_act = conv(xsp_ref, 0, SSM_WIDTH)
    bc_act = conv(bcp_ref, SSM_WIDTH, XBC_DIM)
    hist_x = xsp_ref[PAD_TOP + nv - HIST:PAD_TOP + nv, :]
    hist_bc = bcp_ref[PAD_TOP + nv - HIST:PAD_TOP + nv, :]
    conv_ref[0, :, 0:SSM_WIDTH] = hist_x
    conv_ref[0, :, SSM_WIDTH:XBC_DIM] = hist_bc
    xsp_ref[PAD_TOP - HIST:PAD_TOP, :] = hist_x
    bcp_ref[PAD_TOP - HIST:PAD_TOP, :] = hist_bc

    lane = lax.broadcasted_iota(jnp.int32, (CHUNK, LANES), 1)
    row = lax.broadcasted_iota(jnp.int32, (CHUNK, LANES), 0)
    lane1 = lax.broadcasted_iota(jnp.int32, (1, LANES), 1)
    is_dt = lane < SM_IG
    is_li = (lane >= SM_IG) & (lane < SM_FG)
    is_lf = (lane >= SM_FG) & (lane < SM_FG + ML_HEADS)
    pre = load_rows(sm_ref) + bias_ref[...]
    dt = jnp.where(is_dt, _softplus(pre), 0.0)
    lf = jnp.where(is_lf, -_softplus(-pre), 0.0)
    li = jnp.where(is_li, pre, 0.0)
    if single_row:
        valid = row < nv
        dt = jnp.where(valid, dt, 0.0)
        lf = jnp.where(valid, lf, 0.0)
        li = jnp.where(valid, li, -jnp.inf)
    a_row = jnp.where(lane1 < SM_IG, -jnp.exp(alog_ref[...]), 0.0)
    causal = row >= lane
    tril = jnp.where(causal, 1.0, 0.0).astype(BF16)
    hi, mid, lo = _split3(dt * a_row + lf)
    cs = _dot(tril, hi) + _dot(tril, mid) + _dot(tril, lo)
    cs_t = cs.T
    li_t = li.T
    dt_t = dt.T
    b_last = cs[CHUNK - 1:CHUNK, :]
    ecs = jnp.exp(cs)

    q_all = (load_rows(q_ref) * (ML_DK ** -0.5)).astype(BF16)
    k_all = load_rows(k_ref)
    v_all = load_rows(v_ref).astype(BF16)
    og = load_rows(og_ref)
    m_old = m_ref[0]
    m_new_row = m_old
    y_a_parts = []
    for h in range(ML_HEADS):
        ck = SM_FG + h
        b_col = cs[:, ck:ck + 1]
        b_row = cs_t[ck:ck + 1, :]
        li_row = li_t[SM_IG + h:SM_IG + h + 1, :]
        li_col = li[:, SM_IG + h:SM_IG + h + 1]
        m_prev = m_old[:, h:h + 1]
        dmat = jnp.where(causal, b_col - b_row + li_row, -jnp.inf)
        inter = b_col + m_prev
        m_t = jnp.maximum(inter, jnp.max(dmat, axis=1, keepdims=True))
        w_intra = jnp.exp(dmat - m_t)
        w_inter = jnp.exp(inter - m_t)
        qh = q_all[:, h * ML_DK:(h + 1) * ML_DK]
        kh = k_all[:, h * ML_DK:(h + 1) * ML_DK]
        vh = v_all[:, h * ML_DV:(h + 1) * ML_DV]
        c_old = c_ref[0, h]
        n_old = n_ref[0, h:h + 1, :]
        s = _dot_nt(qh, kh.astype(BF16)) * w_intra
        num = _dot(s.astype(BF16), vh) + w_inter * _dot(qh, c_old.astype(BF16))
        qn = jnp.sum(qh.astype(F32) * n_old, axis=1, keepdims=True)
        den = jnp.sum(s, axis=1, keepdims=True) + w_inter * qn
        hh = num / jnp.maximum(jnp.abs(den), jnp.exp(-m_t))
        m_new = m_t[CHUNK - 1:CHUNK, :]
        bl = b_col[CHUNK - 1:CHUNK, :]
        w_end = jnp.exp(bl - b_col + li_col - m_new)
        decay = jnp.exp(bl + m_prev - m_new)
        kw = kh * w_end
        c_ref[0, h] = decay * c_old + _dot_tn(kw.astype(BF16), vh)
        n_ref[0, h:h + 1, :] = decay * n_old + jnp.sum(kw, axis=0, keepdims=True)
        m_new_row = jnp.where(lane1 == h, m_new, m_new_row)
        hn = _rms(hh) * hnorm_ref[:, h * ML_DV:(h + 1) * ML_DV]
        y_a_parts.append(_sigmoid(og[:, h * ML_DV:(h + 1) * ML_DV]) * hn)
    m_ref[0] = m_new_row
    y_a = jnp.concatenate(y_a_parts, axis=1)

    def expand(x):
        x_hi, x_mid, x_lo = _split3(jnp.where(is_dt, x, 0.0))
        e1 = e1_ref[...]
        return _dot(x_hi, e1) + _dot(x_mid, e1) + _dot(x_lo, e1)

    e_b = expand(ecs)
    e_w = expand(jnp.exp(b_last - cs) * dt)
    xs_bf = xs_act.astype(BF16)
    xw_bf = (xs_act * e_w).astype(BF16)
    half = lax.broadcasted_iota(jnp.int32, (CHUNK, LANES), 1) < SSM_HEADDIM
    gw = SSM_WIDTH // SSM_GROUPS
    hpg = SSM_HEADS // SSM_GROUPS
    y_parts = []
    for g in range(SSM_GROUPS):
        bg = bc_act[:, g * SSM_STATE:(g + 1) * SSM_STATE].astype(BF16)
        cg = bc_act[:, (SSM_GROUPS + g) * SSM_STATE:(SSM_GROUPS + g + 1) * SSM_STATE].astype(BF16)
        cb = _dot_nt(cg, bg)
        s_old = s_ref[0, g * gw:(g + 1) * gw, :]
        y_inter = _dot_nt(cg, s_old.astype(BF16))
        pairs = []
        for j in range(hpg // 2):
            x_pair = xs_bf[:, g * gw + j * LANES:g * gw + (j + 1) * LANES]
            res = []
            for hh_ in range(2):
                h = g * hpg + 2 * j + hh_
                dec = jnp.exp(jnp.where(causal, cs[:, h:h + 1] - cs_t[h:h + 1, :], -jnp.inf))
                mm = (cb * dec * dt_t[h:h + 1, :]).astype(BF16)
                res.append(_dot(mm, x_pair))
            pairs.append(jnp.where(half, res[0], res[1]))
        y_intra = jnp.concatenate(pairs, axis=1)
        y_parts.append(y_intra + e_b[:, g * gw:(g + 1) * gw] * y_inter)
        upd = _dot_tn(xw_bf[:, g * gw:(g + 1) * gw], bg)
        for hh_ in range(hpg):
            h = g * hpg + hh_
            lo_r = g * gw + hh_ * SSM_HEADDIM
            s_ref[0, lo_r:lo_r + SSM_HEADDIM, :] = (
                ecs[CHUNK - 1:CHUNK, h:h + 1] * s_ref[0, lo_r:lo_r + SSM_HEADDIM, :]
                + upd[hh_ * SSM_HEADDIM:(hh_ + 1) * SSM_HEADDIM, :])
    y = jnp.concatenate(y_parts, axis=1)
    y_s = (y + dskip_ref[...] * xs_act) * _silu(load_rows(z_ref))
    y_b = jnp.concatenate(
        [_rms(y_s[:, g * gw:(g + 1) * gw]) for g in range(SSM_GROUPS)], axis=1) * snorm_ref[...]

    gates = load_rows(gates_ref)
    merged = _sigmoid(gates[:, 0:D_MODEL]) * y_a + _sigmoid(gates[:, D_MODEL:2 * D_MODEL]) * y_b
    if single_row:
        merged_ref[pl.ds(r, 1), :] = merged[0:1, :]
    else:
        merged_ref[...] = merged


def _mixer(proj, conv0, c0, n0, m0, s0, bias_row, alog_row, conv_w, conv_b, dskip_row, hnorm, snorm, e1,
           *, n_seq, n_chunks, single_row):
    rows = SUBLANES if single_row else CHUNK
    if single_row:
        def row_blk(b, c):
            return b // SUBLANES
    else:
        def row_blk(b, c):
            return b * n_chunks + c

    def seg(width, start):
        assert start % width == 0
        return pl.BlockSpec((rows, width), lambda b, c: (row_blk(b, c), start // width))

    def const(shape):
        return pl.BlockSpec(shape, lambda b, c: (0,) * len(shape))

    def per_seq(shape):
        return pl.BlockSpec((1,) + shape, lambda b, c: (b,) + (0,) * len(shape))

    m_tot = proj.shape[0]
    state_shapes = [(HIST, XBC_DIM), (ML_HEADS, ML_DK, ML_DV), (ML_HEADS, ML_DK), (1, LANES),
                    (SSM_WIDTH, SSM_STATE)]
    in_specs = [
        seg(2 * D_MODEL, SEG_GATES), seg(D_MODEL, SEG_OG), seg(SSM_WIDTH, SEG_Z), seg(D_MODEL, SEG_V),
        seg(SSM_WIDTH, SEG_XS), seg(ML_QK, SEG_Q), seg(ML_QK, SEG_K), seg(BC_DIM, SEG_BC),
        seg(LANES, SEG_SMALL),
    ] + [per_seq(s) for s in state_shapes] + [
        const((1, LANES)), const((1, LANES)), const((CONV_W, XBC_DIM)), const((1, XBC_DIM)),
        const((1, SSM_WIDTH)), const((1, D_MODEL)), const((1, SSM_WIDTH)), const((LANES, SSM_WIDTH)),
    ]
    out_specs = [pl.BlockSpec((rows, D_MODEL), lambda b, c: (row_blk(b, c), 0))] + [
        per_seq(s) for s in state_shapes]
    out_shape = [jax.ShapeDtypeStruct((m_tot, D_MODEL), F32)] + [
        jax.ShapeDtypeStruct((n_seq,) + s, F32) for s in state_shapes]
    return pl.pallas_call(
        functools.partial(_mixer_kernel, single_row=single_row),
        grid=(n_seq, n_chunks),
        in_specs=in_specs,
        out_specs=out_specs,
        out_shape=out_shape,
        scratch_shapes=[pltpu.VMEM((PAD_TOP + CHUNK, SSM_WIDTH), F32),
                        pltpu.VMEM((PAD_TOP + CHUNK, BC_DIM), F32)],
        compiler_params=pltpu.CompilerParams(
            dimension_semantics=("arbitrary", "arbitrary"), vmem_limit_bytes=VMEM_LIMIT),
        name="mixer_step" if single_row else "mixer_chunk",
    )(*([proj] * 9), conv0, c0, n0, m0, s0,
      bias_row, alog_row, conv_w, conv_b, dskip_row, hnorm, snorm, e1)


def _regroup_w_in(w):
    sizes = [ML_QK, ML_QK, D_MODEL, ML_HEADS, ML_HEADS, D_MODEL, SSM_WIDTH, XBC_DIM, SSM_HEADS, 2 * D_MODEL]
    offs = [0]
    for s in sizes:
        offs.append(offs[-1] + s)
    q, k, v, ig, fg, og, z, xbc, dtr, gates = [w[:, offs[i]:offs[i + 1]] for i in range(len(sizes))]
    small = jnp.concatenate(
        [dtr, ig, fg, jnp.zeros((w.shape[0], LANES - SSM_HEADS - 2 * ML_HEADS), w.dtype)], axis=1)
    out = jnp.concatenate([gates, og, z, v, xbc[:, :SSM_WIDTH], q, k, xbc[:, SSM_WIDTH:], small], axis=1)
    assert out.shape[1] == PROJ_DIM
    return out.astype(BF16)


def _pad_lanes(vec, offset):
    return jnp.pad(vec.astype(F32), (offset, LANES - offset - vec.shape[0])).reshape(1, LANES)


def _trunk(x, states, w, *, single_row, tm, final_w):
    conv0, c0, n0, m0, s0 = states
    n_seq = conv0.shape[0]
    m_rows = x.shape[0]
    n_chunks = 1 if single_row else m_rows // n_seq // CHUNK

    x1 = _ffn(x, w["ffn1_norm"], w["ffn1_w_gate"], w["ffn1_w_up"], w["ffn1_w_down"], final_w,
              final_norm=False, tm=tm, tf=256)
    proj = _inproj(x1, w["mix_norm"], w["w_in_regrouped"], tm=tm, tn=1152)
    m0p = jnp.pad(m0, ((0, 0), (0, LANES - ML_HEADS))).reshape(n_seq, 1, LANES)
    s0r = s0.reshape(n_seq, SSM_WIDTH, SSM_STATE)
    merged, conv1, c1, n1, m1, s1 = _mixer(
        proj, conv0, c0, n0, m0p, s0r, w["bias_row"], w["alog_row"], w["ssm_conv_w"],
        w["ssm_conv_b"].reshape(1, XBC_DIM), w["dskip_row"], w["ml_head_norm"].reshape(1, D_MODEL),
        w["ssm_norm"].reshape(1, SSM_WIDTH), w["e1"],
        n_seq=n_seq, n_chunks=n_chunks, single_row=single_row)
    x2 = _outproj(merged, w["w_out"], x1, tm=tm, tn=512)
    y = _ffn(x2, w["ffn2_norm"], w["ffn2_w_gate"], w["ffn2_w_up"], w["ffn2_w_down"], final_w,
             final_norm=True, tm=tm, tf=256)
    m1 = m1[:, 0, :ML_HEADS]
    s1 = s1.reshape(n_seq, SSM_HEADS, SSM_HEADDIM, SSM_STATE)
    return y, (conv1, c1, n1, m1, s1)


def kernel(x_prompt, x_sample, state_conv, state_mlstm_C, state_mlstm_n, state_mlstm_m, state_ssm, ffn1_norm, ffn1_w_gate, ffn1_w_up, ffn1_w_down, mix_norm, w_in, ml_i_bias, ml_f_bias, ml_head_norm, ssm_conv_w, ssm_conv_b, ssm_dt_bias, ssm_A_log, ssm_D, ssm_norm, w_out, ffn2_norm, ffn2_w_gate, ffn2_w_up, ffn2_w_down, final_norm):
    depth = w_in.shape[0]
    assert depth == 1, "the final norm is fused into the (single) layer's second FFN"
    bp, seq, d = x_prompt.shape
    bs = x_sample.shape[0]
    assert x_sample.shape[1] == 1 and seq % CHUNK == 0 and bs % SUBLANES == 0

    l = 0
    head_of_lane = jnp.arange(SSM_WIDTH) // SSM_HEADDIM
    w = {
        "ffn1_norm": ffn1_norm[l], "ffn1_w_gate": ffn1_w_gate[l], "ffn1_w_up": ffn1_w_up[l],
        "ffn1_w_down": ffn1_w_down[l], "mix_norm": mix_norm[l],
        "w_in_regrouped": _regroup_w_in(w_in[l]),
        "bias_row": (_pad_lanes(ssm_dt_bias[l], SM_DT) + _pad_lanes(ml_i_bias[l], SM_IG)
                     + _pad_lanes(ml_f_bias[l], SM_FG)),
        "alog_row": _pad_lanes(ssm_A_log[l], SM_DT),
        "ml_head_norm": ml_head_norm[l], "ssm_conv_w": ssm_conv_w[l], "ssm_conv_b": ssm_conv_b[l],
        "dskip_row": ssm_D[l].astype(F32)[head_of_lane].reshape(1, SSM_WIDTH),
        "ssm_norm": ssm_norm[l], "w_out": w_out[l],
        "ffn2_norm": ffn2_norm[l], "ffn2_w_gate": ffn2_w_gate[l], "ffn2_w_up": ffn2_w_up[l],
        "ffn2_w_down": ffn2_w_down[l],
        "e1": (jnp.arange(LANES)[:, None] == head_of_lane[None, :]).astype(BF16),
    }

    p_states = (jnp.zeros((bp, HIST, XBC_DIM), F32),
                jnp.zeros((bp, ML_HEADS, ML_DK, ML_DV), F32),
                jnp.zeros((bp, ML_HEADS, ML_DK), F32),
                jnp.full((bp, ML_HEADS), NEG_INIT, F32),
                jnp.zeros((bp, SSM_HEADS, SSM_HEADDIM, SSM_STATE), F32))
    s_states = (state_conv[l], state_mlstm_C[l], state_mlstm_n[l], state_mlstm_m[l], state_ssm[l])

    yp, p_new = _trunk(x_prompt.reshape(bp * seq, d), p_states, w, single_row=False, tm=1024,
                       final_w=final_norm)
    ys, s_new = _trunk(x_sample.reshape(bs, d), s_states, w, single_row=True, tm=bs, final_w=final_norm)

    return (yp.reshape(bp, seq, d), ys.reshape(bs, 1, d),
            *[t[None] for t in p_new], *[t[None] for t in s_new])
```

```python
import functools

import jax
import jax.numpy as jnp
from jax import lax
from jax.experimental import pallas as pl
from jax.experimental.pallas import tpu as pltpu

F32 = jnp.float32
BF16 = jnp.bfloat16

EPS = 1e-6
NEG_INIT = -1e30

D_MODEL = 2048
ML_HEADS = 4
ML_DV = 512
ML_DK = 256
ML_QK = ML_HEADS * ML_DK
SSM_HEADS = 32
SSM_HEADDIM = 64
SSM_GROUPS = 2
SSM_STATE = 128
SSM_WIDTH = SSM_HEADS * SSM_HEADDIM
CONV_W = 4
HIST = CONV_W - 1
BC_DIM = 2 * SSM_GROUPS * SSM_STATE
XBC_DIM = SSM_WIDTH + BC_DIM
CHUNK = 128
IN_SIZES = (ML_QK, ML_QK, D_MODEL, ML_HEADS, ML_HEADS, D_MODEL, SSM_WIDTH, XBC_DIM, SSM_HEADS, 2 * D_MODEL)

LANES = 128
SUBLANES = 8

SEG_GATES = 0
SEG_OG = SEG_GATES + 2 * D_MODEL
SEG_Z = SEG_OG + D_MODEL
SEG_V = SEG_Z + SSM_WIDTH
SEG_XS = SEG_V + D_MODEL
SEG_Q = SEG_XS + SSM_WIDTH
SEG_K = SEG_Q + ML_QK
SEG_BC = SEG_K + ML_QK
SEG_SMALL = SEG_BC + BC_DIM
PROJ_TILE = 1280
PROJ_DIM = -(-(SEG_SMALL + LANES) // PROJ_TILE) * PROJ_TILE
SM_DT = 0
SM_IG = SM_DT + SSM_HEADS
SM_FG = SM_IG + ML_HEADS

VMEM_LIMIT = 56 * 1024 * 1024


def _rms(x):
    return x * lax.rsqrt(jnp.mean(x * x, axis=-1, keepdims=True) + EPS)


def _sigmoid(x):
    return 0.5 * jnp.tanh(0.5 * x) + 0.5


def _silu(x):
    return x * _sigmoid(x)


def _softplus(x):
    return jnp.maximum(x, 0.0) + jnp.log(1.0 + jnp.exp(-jnp.abs(x)))


def _dot(a, b):
    return jnp.dot(a, b, preferred_element_type=F32)


def _dot_nt(a, b):
    return lax.dot_general(a, b, (((1,), (1,)), ((), ())), preferred_element_type=F32)


def _dot_tn(a, b):
    return lax.dot_general(a, b, (((0,), (0,)), ((), ())), preferred_element_type=F32)


def _split3(x):
    hi = x.astype(BF16)
    r1 = x - hi.astype(F32)
    mid = r1.astype(BF16)
    lo = (r1 - mid.astype(F32)).astype(BF16)
    return hi, mid, lo


def _expand_heads(x, e1):
    x_hi, x_mid, x_lo = _split3(x)
    return _dot(x_hi, e1) + _dot(x_mid, e1) + _dot(x_lo, e1)


def _params(dims=2):
    return pltpu.CompilerParams(dimension_semantics=("arbitrary",) * dims, vmem_limit_bytes=VMEM_LIMIT)


def _ffn_kernel(xp_ref, xs_ref, nw_ref, wg_ref, wu_ref, wd_ref, fw_ref, op_ref, os_ref, hp_ref, hs_ref,
                *, final_norm):
    i = pl.program_id(0)
    j = pl.program_id(1)
    last_j = pl.num_programs(1) - 1

    def rows(x_ref, o_ref, h_ref):
        @pl.when(j == 0)
        def _():
            h_ref[...] = (_rms(x_ref[...]) * nw_ref[...]).astype(BF16)
            o_ref[...] = jnp.zeros(o_ref.shape, F32)

        h = h_ref[...]
        g = _dot(h, wg_ref[...].astype(BF16))
        u = _dot(h, wu_ref[...].astype(BF16))
        a = (_silu(g) * u).astype(BF16)
        o_ref[...] += _dot(a, wd_ref[...].astype(BF16))

        @pl.when(j == last_j)
        def _():
            y = x_ref[...] + 0.5 * o_ref[...]
            if final_norm:
                y = _rms(y) * fw_ref[...]
            o_ref[...] = y

    rows(xp_ref, op_ref, hp_ref)

    @pl.when(i == pl.num_programs(0) - 1)
    def _():
        rows(xs_ref, os_ref, hs_ref)


def _ffn(xp, xs, norm_w, w_gate, w_up, w_down, final_w, *, final_norm, tm, tf):
    mp, d = xp.shape
    ms = xs.shape[0]
    f = w_gate.shape[1]
    return pl.pallas_call(
        functools.partial(_ffn_kernel, final_norm=final_norm),
        grid=(mp // tm, f // tf),
        in_specs=[
            pl.BlockSpec((tm, d), lambda i, j: (i, 0), pipeline_mode=pl.Buffered(1)),
            pl.BlockSpec((ms, d), lambda i, j: (0, 0), pipeline_mode=pl.Buffered(1)),
            pl.BlockSpec((1, d), lambda i, j: (0, 0)),
            pl.BlockSpec((d, tf), lambda i, j: (0, j)),
            pl.BlockSpec((d, tf), lambda i, j: (0, j)),
            pl.BlockSpec((tf, d), lambda i, j: (j, 0)),
            pl.BlockSpec((1, d), lambda i, j: (0, 0)),
        ],
        out_specs=[pl.BlockSpec((tm, d), lambda i, j: (i, 0)),
                   pl.BlockSpec((ms, d), lambda i, j: (0, 0))],
        out_shape=[jax.ShapeDtypeStruct((mp, d), F32), jax.ShapeDtypeStruct((ms, d), F32)],
        scratch_shapes=[pltpu.VMEM((tm, d), BF16), pltpu.VMEM((ms, d), BF16)],
        compiler_params=_params(),
        name="ffn",
    )(xp, xs, norm_w.reshape(1, d), w_gate, w_up, w_down, final_w.reshape(1, d))


def _inproj_kernel(xp_ref, xs_ref, nw_ref, w_ref, op_ref, os_ref, up_ref, us_ref):
    i = pl.program_id(0)
    j = pl.program_id(1)

    def rows(x_ref, o_ref, u_ref):
        @pl.when(j == 0)
        def _():
            u_ref[...] = (_rms(x_ref[...]) * nw_ref[...]).astype(BF16)

        o_ref[...] = _dot(u_ref[...], w_ref[...])

    rows(xp_ref, op_ref, up_ref)

    @pl.when(i == pl.num_programs(0) - 1)
    def _():
        rows(xs_ref, os_ref, us_ref)


def _inproj(xp, xs, norm_w, w, *, tm, tn):
    mp, d = xp.shape
    ms = xs.shape[0]
    n = w.shape[1]
    last_i = mp // tm - 1
    return pl.pallas_call(
        _inproj_kernel,
        grid=(mp // tm, n // tn),
        in_specs=[
            pl.BlockSpec((tm, d), lambda i, j: (i, 0), pipeline_mode=pl.Buffered(1)),
            pl.BlockSpec((ms, d), lambda i, j: (0, 0), pipeline_mode=pl.Buffered(1)),
            pl.BlockSpec((1, d), lambda i, j: (0, 0)),
            pl.BlockSpec((d, tn), lambda i, j: (0, j)),
        ],
        out_specs=[pl.BlockSpec((tm, tn), lambda i, j: (i, j)),
                   pl.BlockSpec((ms, tn), lambda i, j: (0, jnp.where(i == last_i, j, 0)))],
        out_shape=[jax.ShapeDtypeStruct((mp, n), F32), jax.ShapeDtypeStruct((ms, n), F32)],
        scratch_shapes=[pltpu.VMEM((tm, d), BF16), pltpu.VMEM((ms, d), BF16)],
        compiler_params=_params(),
        name="inproj",
    )(xp, xs, norm_w.reshape(1, d), w)


def _outproj_kernel(yp_ref, ys_ref, w_ref, xp_ref, xs_ref, op_ref, os_ref):
    w = w_ref[...].astype(BF16)
    op_ref[...] = xp_ref[...] + _dot(yp_ref[...].astype(BF16), w)

    @pl.when(pl.program_id(0) == pl.num_programs(0) - 1)
    def _():
        os_ref[...] = xs_ref[...] + _dot(ys_ref[...].astype(BF16), w)


def _outproj(yp, ys, w, xp, xs, *, tm, tn):
    mp, d = yp.shape
    ms = ys.shape[0]
    n = w.shape[1]
    last_i = mp // tm - 1

    def sample_tile(i, j):
        return (0, jnp.where(i == last_i, j, 0))

    return pl.pallas_call(
        _outproj_kernel,
        grid=(mp // tm, n // tn),
        in_specs=[
            pl.BlockSpec((tm, d), lambda i, j: (i, 0)),
            pl.BlockSpec((ms, d), lambda i, j: (0, 0)),
            pl.BlockSpec((d, tn), lambda i, j: (0, j)),
            pl.BlockSpec((tm, tn), lambda i, j: (i, j)),
            pl.BlockSpec((ms, tn), sample_tile),
        ],
        out_specs=[pl.BlockSpec((tm, tn), lambda i, j: (i, j)),
                   pl.BlockSpec((ms, tn), sample_tile)],
        out_shape=[jax.ShapeDtypeStruct((mp, n), F32), jax.ShapeDtypeStruct((ms, n), F32)],
        compiler_params=_params(),
        name="outproj",
    )(yp, ys, w, xp, xs)


def _seg_sources():
    offs = [0]
    for s in IN_SIZES:
        offs.append(offs[-1] + s)
    q, k, v, ig, fg, og, z, xbc, dtr, gates = offs[:-1]
    return [(SEG_GATES, gates, 2 * D_MODEL), (SEG_OG, og, D_MODEL), (SEG_Z, z, SSM_WIDTH),
            (SEG_V, v, D_MODEL), (SEG_XS, xbc, SSM_WIDTH), (SEG_Q, q, ML_QK), (SEG_K, k, ML_QK),
            (SEG_BC, xbc + SSM_WIDTH, BC_DIM), (SEG_SMALL + SM_DT, dtr, SSM_HEADS),
            (SEG_SMALL + SM_IG, ig, ML_HEADS), (SEG_SMALL + SM_FG, fg, ML_HEADS)]


def _regroup_kernel(w_ref, o_ref):
    rows = w_ref.shape[0]
    n_in = w_ref.shape[1]
    o_ref[:, SEG_SMALL:PROJ_DIM] = jnp.zeros((rows, PROJ_DIM - SEG_SMALL), BF16)
    lane = lax.broadcasted_iota(jnp.int32, (rows, LANES), 1)
    small = jnp.zeros((rows, LANES), F32)
    for dst, src, width in _seg_sources():
        lo = src // LANES * LANES
        if dst < SEG_SMALL:
            hi = min(-(-(src + width) // LANES) * LANES, n_in)
            o_ref[:, dst:dst + width] = w_ref[:, lo:hi][:, src - lo:src - lo + width].astype(BF16)
        else:
            d0 = dst - SEG_SMALL
            assert src - lo + width <= LANES
            tile = pltpu.roll(w_ref[:, lo:lo + LANES], (d0 - (src - lo)) % LANES, axis=1)
            small = jnp.where((lane >= d0) & (lane < d0 + width), tile, small)
    o_ref[:, SEG_SMALL:SEG_SMALL + LANES] = small.astype(BF16)


def _regroup_w_in(w, *, tr):
    d, n_in = w.shape
    return pl.pallas_call(
        _regroup_kernel,
        grid=(d // tr,),
        in_specs=[pl.BlockSpec((tr, n_in), lambda i: (i, 0))],
        out_specs=pl.BlockSpec((tr, PROJ_DIM), lambda i: (i, 0)),
        out_shape=jax.ShapeDtypeStruct((d, PROJ_DIM), BF16),
        compiler_params=_params(1),
        name="regroup",
    )(w)


PAD_TOP = SUBLANES


def _mixer_kernel(gates_ref, og_ref, z_ref, v_ref, xs_ref, q_ref, k_ref, bc_ref, sm_ref,
                  bias_ref, alog_ref, convw_ref, convb_ref, dskip_ref, hnorm_ref, snorm_ref, e1_ref,
                  merged_ref, conv_ref, c_ref, n_ref, m_ref, s_ref,
                  xsp_ref, bcp_ref):
    c = pl.program_id(1)

    @pl.when(c == 0)
    def _():
        c_ref[...] = jnp.zeros(c_ref.shape, F32)
        n_ref[...] = jnp.zeros(n_ref.shape, F32)
        m_ref[...] = jnp.full(m_ref.shape, NEG_INIT, F32)
        s_ref[...] = jnp.zeros(s_ref.shape, F32)
        xsp_ref[0:PAD_TOP, :] = jnp.zeros((PAD_TOP, SSM_WIDTH), F32)
        bcp_ref[0:PAD_TOP, :] = jnp.zeros((PAD_TOP, BC_DIM), F32)

    xsp_ref[PAD_TOP:PAD_TOP + CHUNK, :] = xs_ref[...]
    bcp_ref[PAD_TOP:PAD_TOP + CHUNK, :] = bc_ref[...]

    def conv(p_ref, lo, hi):
        acc = p_ref[PAD_TOP - HIST:PAD_TOP - HIST + CHUNK, :] * convw_ref[0:1, lo:hi]
        for j in range(1, CONV_W):
            acc = acc + p_ref[PAD_TOP - HIST + j:PAD_TOP - HIST + j + CHUNK, :] * convw_ref[j:j + 1, lo:hi]
        return _silu(acc + convb_ref[:, lo:hi])

    xs_act = conv(xsp_ref, 0, SSM_WIDTH)
    bc_act = conv(bcp_ref, SSM_WIDTH, XBC_DIM)
    hist_x = xsp_ref[PAD_TOP + CHUNK - HIST:PAD_TOP + CHUNK, :]
    hist_bc = bcp_ref[PAD_TOP + CHUNK - HIST:PAD_TOP + CHUNK, :]
    conv_ref[0, :, 0:SSM_WIDTH] = hist_x
    conv_ref[0, :, SSM_WIDTH:XBC_DIM] = hist_bc
    xsp_ref[PAD_TOP - HIST:PAD_TOP, :] = hist_x
    bcp_ref[PAD_TOP - HIST:PAD_TOP, :] = hist_bc

    lane = lax.broadcasted_iota(jnp.int32, (CHUNK, LANES), 1)
    row = lax.broadcasted_iota(jnp.int32, (CHUNK, LANES), 0)
    lane1 = lax.broadcasted_iota(jnp.int32, (1, LANES), 1)
    is_dt = lane < SM_IG
    is_li = (lane >= SM_IG) & (lane < SM_FG)
    is_lf = (lane >= SM_FG) & (lane < SM_FG + ML_HEADS)
    pre = sm_ref[...] + bias_ref[...]
    dt = jnp.where(is_dt, _softplus(pre), 0.0)
    lf = jnp.where(is_lf, -_softplus(-pre), 0.0)
    li = jnp.where(is_li, pre, 0.0)
    a_row = jnp.where(lane1 < SM_IG, -jnp.exp(alog_ref[...]), 0.0)
    causal = row >= lane
    tril = jnp.where(causal, 1.0, 0.0).astype(BF16)
    hi, mid, lo = _split3(dt * a_row + lf)
    cs = _dot(tril, hi) + _dot(tril, mid) + _dot(tril, lo)
    cs_t = cs.T
    li_t = li.T
    dt_t = dt.T
    b_last = cs[CHUNK - 1:CHUNK, :]
    ecs = jnp.exp(cs)

    q_all = (q_ref[...] * (ML_DK ** -0.5)).astype(BF16)
    k_all = k_ref[...]
    v_all = v_ref[...].astype(BF16)
    og = og_ref[...]
    m_old = m_ref[0]
    m_new_row = m_old
    y_a_parts = []
    for h in range(ML_HEADS):
        ck = SM_FG + h
        b_col = cs[:, ck:ck + 1]
        b_row = cs_t[ck:ck + 1, :]
        li_row = li_t[SM_IG + h:SM_IG + h + 1, :]
        li_col = li[:, SM_IG + h:SM_IG + h + 1]
        m_prev = m_old[:, h:h + 1]
        dmat = jnp.where(causal, b_col - b_row + li_row, -jnp.inf)
        inter = b_col + m_prev
        m_t = jnp.maximum(inter, jnp.max(dmat, axis=1, keepdims=True))
        w_intra = jnp.exp(dmat - m_t)
        w_inter = jnp.exp(inter - m_t)
        qh = q_all[:, h * ML_DK:(h + 1) * ML_DK]
        kh = k_all[:, h * ML_DK:(h + 1) * ML_DK]
        vh = v_all[:, h * ML_DV:(h + 1) * ML_DV]
        c_old = c_ref[0, h]
        n_old = n_ref[0, h:h + 1, :]
        s = _dot_nt(qh, kh.astype(BF16)) * w_intra
        num = _dot(s.astype(BF16), vh) + w_inter * _dot(qh, c_old.astype(BF16))
        qn = jnp.sum(qh.astype(F32) * n_old, axis=1, keepdims=True)
        den = jnp.sum(s, axis=1, keepdims=True) + w_inter * qn
        hh = num / jnp.maximum(jnp.abs(den), jnp.exp(-m_t))
        m_new = m_t[CHUNK - 1:CHUNK, :]
        bl = b_col[CHUNK - 1:CHUNK, :]
        w_end = jnp.exp(bl - b_col + li_col - m_new)
        decay = jnp.exp(bl + m_prev - m_new)
        kw = kh * w_end
        c_ref[0, h] = decay * c_old + _dot_tn(kw.astype(BF16), vh)
        n_ref[0, h:h + 1, :] = decay * n_old + jnp.sum(kw, axis=0, keepdims=True)
        m_new_row = jnp.where(lane1 == h, m_new, m_new_row)
        hn = _rms(hh) * hnorm_ref[:, h * ML_DV:(h + 1) * ML_DV]
        y_a_parts.append(_sigmoid(og[:, h * ML_DV:(h + 1) * ML_DV]) * hn)
    m_ref[0] = m_new_row
    y_a = jnp.concatenate(y_a_parts, axis=1)

    e1 = e1_ref[...]
    e_b = _expand_heads(jnp.where(is_dt, ecs, 0.0), e1)
    e_w = _expand_heads(jnp.exp(b_last - cs) * dt, e1)
    xs_bf = xs_act.astype(BF16)
    xw_bf = (xs_act * e_w).astype(BF16)
    half = lane < SSM_HEADDIM
    gw = SSM_WIDTH // SSM_GROUPS
    hpg = SSM_HEADS // SSM_GROUPS
    y_parts = []
    for g in range(SSM_GROUPS):
        bg = bc_act[:, g * SSM_STATE:(g + 1) * SSM_STATE].astype(BF16)
        cg = bc_act[:, (SSM_GROUPS + g) * SSM_STATE:(SSM_GROUPS + g + 1) * SSM_STATE].astype(BF16)
        cb = _dot_nt(cg, bg)
        s_old = s_ref[0, g * gw:(g + 1) * gw, :]
        y_inter = _dot_nt(cg, s_old.astype(BF16))
        pairs = []
        for j in range(hpg // 2):
            x_pair = xs_bf[:, g * gw + j * LANES:g * gw + (j + 1) * LANES]
            res = []
            for hh_ in range(2):
                h = g * hpg + 2 * j + hh_
                dec = jnp.exp(jnp.where(causal, cs[:, h:h + 1] - cs_t[h:h + 1, :], -jnp.inf))
                mm = (cb * dec * dt_t[h:h + 1, :]).astype(BF16)
                res.append(_dot(mm, x_pair))
            pairs.append(jnp.where(half, res[0], res[1]))
        y_intra = jnp.concatenate(pairs, axis=1)
        y_parts.append(y_intra + e_b[:, g * gw:(g + 1) * gw] * y_inter)
        upd = _dot_tn(xw_bf[:, g * gw:(g + 1) * gw], bg)
        for hh_ in range(hpg):
            h = g * hpg + hh_
            lo_r = g * gw + hh_ * SSM_HEADDIM
            s_ref[0, lo_r:lo_r + SSM_HEADDIM, :] = (
                ecs[CHUNK - 1:CHUNK, h:h + 1] * s_ref[0, lo_r:lo_r + SSM_HEADDIM, :]
                + upd[hh_ * SSM_HEADDIM:(hh_ + 1) * SSM_HEADDIM, :])
    y = jnp.concatenate(y_parts, axis=1)
    y_s = (y + dskip_ref[...] * xs_act) * _silu(z_ref[...])
    y_b = jnp.concatenate(
        [_rms(y_s[:, g * gw:(g + 1) * gw]) for g in range(SSM_GROUPS)], axis=1) * snorm_ref[...]

    gates = gates_ref[...]
    merged_ref[...] = (_sigmoid(gates[:, 0:D_MODEL]) * y_a
                       + _sigmoid(gates[:, D_MODEL:2 * D_MODEL]) * y_b)


def _const_spec(shape, single_buffer=False):
    kwargs = dict(pipeline_mode=pl.Buffered(1)) if single_buffer else {}
    return pl.BlockSpec(shape, lambda *_: (0,) * len(shape), **kwargs)


def _mixer_params(w):
    return [w["bias_row"], w["alog_row"], w["ssm_conv_w"], w["ssm_conv_b"].reshape(1, XBC_DIM),
            w["dskip_row"], w["ml_head_norm"].reshape(1, D_MODEL), w["ssm_norm"].reshape(1, SSM_WIDTH),
            w["e1"]]


_MIXER_PARAM_SHAPES = [(1, LANES), (1, LANES), (CONV_W, XBC_DIM), (1, XBC_DIM), (1, SSM_WIDTH),
                       (1, D_MODEL), (1, SSM_WIDTH), (LANES, SSM_WIDTH)]


def _mixer_chunk(proj, w, *, n_seq, n_chunks):
    def seg(width, start):
        assert start % width == 0
        return pl.BlockSpec((CHUNK, width), lambda b, c: (b * n_chunks + c, start // width))

    def per_seq(shape):
        return pl.BlockSpec((1,) + shape, lambda b, c: (b,) + (0,) * len(shape))

    state_shapes = [(HIST, XBC_DIM), (ML_HEADS, ML_DK, ML_DV), (ML_HEADS, ML_DK), (1, LANES),
                    (SSM_WIDTH, SSM_STATE)]
    in_specs = [
        seg(2 * D_MODEL, SEG_GATES), seg(D_MODEL, SEG_OG), seg(SSM_WIDTH, SEG_Z), seg(D_MODEL, SEG_V),
        seg(SSM_WIDTH, SEG_XS), seg(ML_QK, SEG_Q), seg(ML_QK, SEG_K), seg(BC_DIM, SEG_BC),
        seg(LANES, SEG_SMALL),
    ] + [_const_spec(s) for s in _MIXER_PARAM_SHAPES]
    out_specs = [pl.BlockSpec((CHUNK, D_MODEL), lambda b, c: (b * n_chunks + c, 0))] + [
        per_seq(s) for s in state_shapes]
    out_shape = [jax.ShapeDtypeStruct((proj.shape[0], D_MODEL), F32)] + [
        jax.ShapeDtypeStruct((n_seq,) + s, F32) for s in state_shapes]
    return pl.pallas_call(
        _mixer_kernel,
        grid=(n_seq, n_chunks),
        in_specs=in_specs,
        out_specs=out_specs,
        out_shape=out_shape,
        scratch_shapes=[pltpu.VMEM((PAD_TOP + CHUNK, SSM_WIDTH), F32),
                        pltpu.VMEM((PAD_TOP + CHUNK, BC_DIM), F32)],
        compiler_params=_params(),
        name="mixer_chunk",
    )(*([proj] * 9), *_mixer_params(w))


def _step_kernel(proj_ref, conv0_ref, n0_ref, m0_ref, c0_ref, s0_ref,
                 bias_ref, alog_ref, convw_ref, convb_ref, dskip_ref, hnorm_ref, snorm_ref, e1_ref,
                 merged_ref, conv_ref, n_ref, m_ref, c_ref, s_ref,
                 qf_ref, kt_ref, vb_ref, cmf_ref, bb_ref, xwt_ref, dec_ref, accq_ref, accs_ref):
    b = pl.program_id(0)
    ns = proj_ref.shape[0]
    gw = SSM_WIDTH // SSM_GROUPS
    hpg = SSM_HEADS // SSM_GROUPS
    lane = lax.broadcasted_iota(jnp.int32, (ns, LANES), 1)
    lane1 = lax.broadcasted_iota(jnp.int32, (1, LANES), 1)
    is_dt = lane < SM_IG
    is_li = (lane >= SM_IG) & (lane < SM_FG)
    is_lf = (lane >= SM_FG) & (lane < SM_FG + ML_HEADS)

    def gate_scalars():
        pre = proj_ref[:, SEG_SMALL:SEG_SMALL + LANES] + bias_ref[...]
        dt = jnp.where(is_dt, _softplus(pre), 0.0)
        lf = pltpu.roll(jnp.where(is_lf, -_softplus(-pre), 0.0), LANES - ML_HEADS, axis=1)
        li = jnp.where(is_li, pre, 0.0)
        inter = lf + m0_ref[...]
        m_t = jnp.maximum(inter, li)
        a_row = jnp.where(lane1 < SM_IG, -jnp.exp(alog_ref[...]), 0.0)
        return dict(dt=dt, m_t=m_t, w_intra=jnp.exp(li - m_t), w_inter=jnp.exp(inter - m_t),
                    e=jnp.exp(dt * a_row))

    def conv_act(lo, hi, seg, store):
        x_new = proj_ref[:, seg:seg + hi - lo]
        st = [conv0_ref[:, j * XBC_DIM + lo:j * XBC_DIM + hi] for j in range(HIST)]
        acc = st[0] * convw_ref[0:1, lo:hi]
        for j in range(1, HIST):
            acc = acc + st[j] * convw_ref[j:j + 1, lo:hi]
        acc = acc + x_new * convw_ref[HIST:HIST + 1, lo:hi]
        if store:
            for j in range(1, HIST):
                conv_ref[:, (j - 1) * XBC_DIM + lo:(j - 1) * XBC_DIM + hi] = st[j]
            conv_ref[:, (HIST - 1) * XBC_DIM + lo:(HIST - 1) * XBC_DIM + hi] = x_new
        return _silu(acc + convb_ref[:, lo:hi])

    def head_col(x, h):
        return x[:, SM_IG + h:SM_IG + h + 1]

    @pl.when(b == 0)
    def _():
        gs = gate_scalars()
        m_ref[...] = gs["m_t"]
        dec_ref[...] = jnp.where(is_dt, gs["e"], gs["w_inter"])
        qf_ref[...] = (proj_ref[:, SEG_Q:SEG_Q + ML_QK] * (ML_DK ** -0.5)).astype(BF16).astype(F32)
        vb_ref[...] = proj_ref[:, SEG_V:SEG_V + D_MODEL].astype(BF16)
        for h in range(ML_HEADS):
            sl = slice(h * ML_DK, (h + 1) * ML_DK)
            kw = proj_ref[:, SEG_K + h * ML_DK:SEG_K + (h + 1) * ML_DK] * head_col(gs["w_intra"], h)
            n_ref[:, sl] = head_col(gs["w_inter"], h) * n0_ref[:, sl] + kw
            kt_ref[sl, :] = kw.T
        xs_act = conv_act(0, SSM_WIDTH, SEG_XS, True)
        bc_act = conv_act(SSM_WIDTH, XBC_DIM, SEG_BC, True)
        bb_ref[...] = bc_act[:, 0:SSM_GROUPS * SSM_STATE].astype(BF16)
        cmf_ref[...] = bc_act[:, SSM_GROUPS * SSM_STATE:BC_DIM]
        xw = xs_act * _expand_heads(gs["dt"], e1_ref[...])
        for g in range(SSM_GROUPS):
            xwt_ref[g * gw:(g + 1) * gw, :] = xw[:, g * gw:(g + 1) * gw].T
        accq_ref[...] = jnp.zeros(accq_ref.shape, F32)
        accs_ref[...] = jnp.zeros(accs_ref.shape, F32)

    base = pl.multiple_of((b // SUBLANES) * SUBLANES, SUBLANES)
    is_row = lax.broadcasted_iota(jnp.int32, (SUBLANES, 1), 0) == b % SUBLANES
    is_seq = lax.broadcasted_iota(jnp.int32, (1, ns), 1) == b
    q8 = jnp.where(is_row, qf_ref[pl.ds(base, SUBLANES), :], 0.0).astype(BF16)
    c8 = jnp.where(is_row, cmf_ref[pl.ds(base, SUBLANES), :], 0.0).astype(BF16)
    drow = dec_ref[pl.ds(b, 1), :]
    for h in range(ML_HEADS):
        c_old = c0_ref[0, h]
        accq_ref[pl.ds(base, SUBLANES), h * ML_DV:(h + 1) * ML_DV] += _dot(
            q8[:, h * ML_DK:(h + 1) * ML_DK], c_old.astype(BF16))
        kt = jnp.where(is_seq, kt_ref[h * ML_DK:(h + 1) * ML_DK, :], 0.0).astype(BF16)
        c_ref[0, h] = (drow[:, SM_IG + h:SM_IG + h + 1] * c_old
                       + _dot(kt, vb_ref[:, h * ML_DV:(h + 1) * ML_DV]))
    for g in range(SSM_GROUPS):
        s_old = s0_ref[0, g * gw:(g + 1) * gw, :]
        accs_ref[pl.ds(base, SUBLANES), g * gw:(g + 1) * gw] += _dot_nt(
            c8[:, g * SSM_STATE:(g + 1) * SSM_STATE], s_old.astype(BF16))
        xwt = jnp.where(is_seq, xwt_ref[g * gw:(g + 1) * gw, :], 0.0).astype(BF16)
        upd = _dot(xwt, bb_ref[:, g * SSM_STATE:(g + 1) * SSM_STATE])
        for hh_ in range(hpg):
            h = g * hpg + hh_
            rs = slice(hh_ * SSM_HEADDIM, (hh_ + 1) * SSM_HEADDIM)
            s_ref[0, g * gw + hh_ * SSM_HEADDIM:g * gw + (hh_ + 1) * SSM_HEADDIM, :] = (
                drow[:, h:h + 1] * s_old[rs, :] + upd[rs, :])

    @pl.when(b == pl.num_programs(0) - 1)
    def _():
        gs = gate_scalars()
        q = qf_ref[...]
        y_a_parts = []
        for h in range(ML_HEADS):
            qh = q[:, h * ML_DK:(h + 1) * ML_DK]
            kh = proj_ref[:, SEG_K + h * ML_DK:SEG_K + (h + 1) * ML_DK].astype(BF16).astype(F32)
            vh = vb_ref[:, h * ML_DV:(h + 1) * ML_DV].astype(F32)
            w_inter = head_col(gs["w_inter"], h)
            s = jnp.sum(qh * kh, axis=1, keepdims=True) * head_col(gs["w_intra"], h)
            qn = jnp.sum(qh * n0_ref[:, h * ML_DK:(h + 1) * ML_DK], axis=1, keepdims=True)
            num = s.astype(BF16).astype(F32) * vh + w_inter * accq_ref[:, h * ML_DV:(h + 1) * ML_DV]
            den = s + w_inter * qn
            hh = num / jnp.maximum(jnp.abs(den), jnp.exp(-head_col(gs["m_t"], h)))
            hn = _rms(hh) * hnorm_ref[:, h * ML_DV:(h + 1) * ML_DV]
            y_a_parts.append(_sigmoid(proj_ref[:, SEG_OG + h * ML_DV:SEG_OG + (h + 1) * ML_DV]) * hn)
        y_a = jnp.concatenate(y_a_parts, axis=1)

        xs_act = conv_act(0, SSM_WIDTH, SEG_XS, False)
        bb = bb_ref[...].astype(F32)
        cm = cmf_ref[...].astype(BF16).astype(F32)
        e1 = e1_ref[...]
        x_dt = xs_act * _expand_heads(gs["dt"], e1)
        e_e = _expand_heads(jnp.where(is_dt, gs["e"], 0.0), e1)
        y_parts = []
        for g in range(SSM_GROUPS):
            gl = slice(g * SSM_STATE, (g + 1) * SSM_STATE)
            cb = jnp.sum(cm[:, gl] * bb[:, gl], axis=1, keepdims=True)
            cl = slice(g * gw, (g + 1) * gw)
            y_parts.append(cb * x_dt[:, cl] + e_e[:, cl] * accs_ref[:, cl])
        y = jnp.concatenate(y_parts, axis=1)
        y_s = (y + dskip_ref[...] * xs_act) * _silu(proj_ref[:, SEG_Z:SEG_Z + SSM_WIDTH])
        y_b = jnp.concatenate(
            [_rms(y_s[:, g * gw:(g + 1) * gw]) for g in range(SSM_GROUPS)], axis=1) * snorm_ref[...]
        merged_ref[...] = (_sigmoid(proj_ref[:, SEG_GATES:SEG_GATES + D_MODEL]) * y_a
                           + _sigmoid(proj_ref[:, SEG_GATES + D_MODEL:SEG_GATES + 2 * D_MODEL]) * y_b)


def _mixer_step(proj, conv0, n0, m0, c0, s0, w):
    ns = proj.shape[0]

    def per_seq(shape):
        return pl.BlockSpec((1,) + shape, lambda b: (b,) + (0,) * len(shape))

    row_shapes = [(ns, HIST * XBC_DIM), (ns, ML_QK), (ns, LANES)]
    mat_shapes = [(ML_HEADS, ML_DK, ML_DV), (SSM_WIDTH, SSM_STATE)]
    in_specs = ([_const_spec((ns, PROJ_DIM), True)] + [_const_spec(s, True) for s in row_shapes]
                + [per_seq(s) for s in mat_shapes] + [_const_spec(s) for s in _MIXER_PARAM_SHAPES])
    out_specs = ([_const_spec((ns, D_MODEL))] + [_const_spec(s) for s in row_shapes]
                 + [per_seq(s) for s in mat_shapes])
    out_shape = ([jax.ShapeDtypeStruct((ns, D_MODEL), F32)]
                 + [jax.ShapeDtypeStruct(s, F32) for s in row_shapes]
                 + [jax.ShapeDtypeStruct((ns,) + s, F32) for s in mat_shapes])
    scratch = [
        pltpu.VMEM((ns, ML_QK), F32),
        pltpu.VMEM((ML_QK, ns), F32),
        pltpu.VMEM((ns, D_MODEL), BF16),
        pltpu.VMEM((ns, SSM_GROUPS * SSM_STATE), F32),
        pltpu.VMEM((ns, SSM_GROUPS * SSM_STATE), BF16),
        pltpu.VMEM((SSM_WIDTH, ns), F32),
        pltpu.VMEM((ns, LANES), F32),
        pltpu.VMEM((ns, D_MODEL), F32),
        pltpu.VMEM((ns, SSM_WIDTH), F32),
    ]
    return pl.pallas_call(
        _step_kernel,
        grid=(ns,),
        in_specs=in_specs,
        out_specs=out_specs,
        out_shape=out_shape,
        scratch_shapes=scratch,
        compiler_params=_params(1),
        name="mixer_step",
    )(proj, conv0, n0, m0, c0, s0, *_mixer_params(w))


def _pad_lanes(vec, offset):
    pad = [(0, 0)] * (vec.ndim - 1) + [(offset, LANES - offset - vec.shape[-1])]
    return jnp.pad(vec.astype(F32), pad)


def kernel(x_prompt, x_sample, state_conv, state_mlstm_C, state_mlstm_n, state_mlstm_m, state_ssm, ffn1_norm, ffn1_w_gate, ffn1_w_up, ffn1_w_down, mix_norm, w_in, ml_i_bias, ml_f_bias, ml_head_norm, ssm_conv_w, ssm_conv_b, ssm_dt_bias, ssm_A_log, ssm_D, ssm_norm, w_out, ffn2_norm, ffn2_w_gate, ffn2_w_up, ffn2_w_down, final_norm):
    depth = w_in.shape[0]
    assert depth == 1, "the final norm is fused into the (single) layer's second FFN"
    bp, seq, d = x_prompt.shape
    bs = x_sample.shape[0]
    assert x_sample.shape[1] == 1 and seq % CHUNK == 0 and bs % SUBLANES == 0
    l = 0
    tm = 1024

    head_of_lane = jnp.arange(SSM_WIDTH) // SSM_HEADDIM
    w = {
        "bias_row": (_pad_lanes(ssm_dt_bias[l], SM_DT) + _pad_lanes(ml_i_bias[l], SM_IG)
                     + _pad_lanes(ml_f_bias[l], SM_FG)).reshape(1, LANES),
        "alog_row": _pad_lanes(ssm_A_log[l], SM_DT).reshape(1, LANES),
        "ml_head_norm": ml_head_norm[l], "ssm_conv_w": ssm_conv_w[l], "ssm_conv_b": ssm_conv_b[l],
        "dskip_row": ssm_D[l].astype(F32)[head_of_lane].reshape(1, SSM_WIDTH),
        "ssm_norm": ssm_norm[l],
        "e1": (jnp.arange(LANES)[:, None] == head_of_lane[None, :]).astype(BF16),
    }

    xp = x_prompt.reshape(bp * seq, d)
    xs = x_sample.reshape(bs, d)
    xp1, xs1 = _ffn(xp, xs, ffn1_norm[l], ffn1_w_gate[l], ffn1_w_up[l], ffn1_w_down[l], final_norm,
                    final_norm=False, tm=tm, tf=256)
    w_in_r = _regroup_w_in(w_in[l], tr=128)
    proj_p, proj_s = _inproj(xp1, xs1, mix_norm[l], w_in_r, tm=tm, tn=PROJ_TILE)

    mp, p_conv, p_c, p_n, p_m, p_s = _mixer_chunk(proj_p, w, n_seq=bp, n_chunks=seq // CHUNK)
    ms, s_conv, s_n, s_m, s_c, s_s = _mixer_step(
        proj_s, state_conv[l].reshape(bs, HIST * XBC_DIM), state_mlstm_n[l].reshape(bs, ML_QK),
        _pad_lanes(state_mlstm_m[l], SM_IG), state_mlstm_C[l],
        state_ssm[l].reshape(bs, SSM_WIDTH, SSM_STATE), w)

    xp2, xs2 = _outproj(mp, ms, w_out[l], xp1, xs1, tm=tm, tn=512)
    yp, ys = _ffn(xp2, xs2, ffn2_norm[l], ffn2_w_gate[l], ffn2_w_up[l], ffn2_w_down[l], final_norm,
                  final_norm=True, tm=tm, tf=256)

    ssm_shape = (SSM_HEADS, SSM_HEADDIM, SSM_STATE)
    return (yp.reshape(bp, seq, d), ys.reshape(bs, 1, d),
            p_conv[None], p_c[None], p_n[None], p_m[:, 0, :ML_HEADS][None], p_s.reshape(bp, *ssm_shape)[None],
            s_conv.reshape(bs, HIST, XBC_DIM)[None], s_c[None], s_n.reshape(bs, ML_HEADS, ML_DK)[None],
            s_m[:, SM_IG:SM_IG + ML_HEADS][None], s_s.reshape(bs, *ssm_shape)[None])
```

```python
import functools

import jax
import jax.numpy as jnp
from jax import lax
from jax.experimental import pallas as pl
from jax.experimental.pallas import tpu as pltpu

F32 = jnp.float32
BF16 = jnp.bfloat16

EPS = 1e-6
NEG_INIT = -1e30

D_MODEL = 2048
ML_HEADS = 4
ML_DV = 512
ML_DK = 256
ML_QK = ML_HEADS * ML_DK
SSM_HEADS = 32
SSM_HEADDIM = 64
SSM_GROUPS = 2
SSM_STATE = 128
SSM_WIDTH = SSM_HEADS * SSM_HEADDIM
CONV_W = 4
HIST = CONV_W - 1
BC_DIM = 2 * SSM_GROUPS * SSM_STATE
XBC_DIM = SSM_WIDTH + BC_DIM
CHUNK = 128
IN_SIZES = (ML_QK, ML_QK, D_MODEL, ML_HEADS, ML_HEADS, D_MODEL, SSM_WIDTH, XBC_DIM, SSM_HEADS, 2 * D_MODEL)

LANES = 128
SUBLANES = 8

SEG_GATES = 0
SEG_OG = SEG_GATES + 2 * D_MODEL
SEG_Z = SEG_OG + D_MODEL
SEG_V = SEG_Z + SSM_WIDTH
SEG_XS = SEG_V + D_MODEL
SEG_Q = SEG_XS + SSM_WIDTH
SEG_K = SEG_Q + ML_QK
SEG_BC = SEG_K + ML_QK
SEG_SMALL = SEG_BC + BC_DIM
PROJ_TILE = 1280
PROJ_DIM = -(-(SEG_SMALL + LANES) // PROJ_TILE) * PROJ_TILE
SM_DT = 0
SM_IG = SM_DT + SSM_HEADS
SM_FG = SM_IG + ML_HEADS

VMEM_LIMIT = 60 * 1024 * 1024


def _rms(x):
    return x * lax.rsqrt(jnp.mean(x * x, axis=-1, keepdims=True) + EPS)


def _sigmoid(x):
    return 0.5 * jnp.tanh(0.5 * x) + 0.5


def _silu(x):
    return x * _sigmoid(x)


def _softplus(x):
    return jnp.maximum(x, 0.0) + jnp.log(1.0 + jnp.exp(-jnp.abs(x)))


def _dot(a, b):
    return jnp.dot(a, b, preferred_element_type=F32)


def _dot_nt(a, b):
    return lax.dot_general(a, b, (((1,), (1,)), ((), ())), preferred_element_type=F32)


def _dot_tn(a, b):
    return lax.dot_general(a, b, (((0,), (0,)), ((), ())), preferred_element_type=F32)


def _split3(x):
    hi = x.astype(BF16)
    r1 = x - hi.astype(F32)
    mid = r1.astype(BF16)
    lo = (r1 - mid.astype(F32)).astype(BF16)
    return hi, mid, lo


def _expand_heads(x, e1):
    x_hi, x_mid, x_lo = _split3(x)
    return _dot(x_hi, e1) + _dot(x_mid, e1) + _dot(x_lo, e1)


def _params(dims=2):
    return pltpu.CompilerParams(dimension_semantics=("arbitrary",) * dims, vmem_limit_bytes=VMEM_LIMIT)


def _ffn_kernel(xp_ref, xs_ref, nw_ref, wg_ref, wu_ref, wd_ref, fw_ref, op_ref, os_ref, hp_ref, hs_ref,
                *, final_norm):
    i = pl.program_id(0)
    j = pl.program_id(1)
    last_j = pl.num_programs(1) - 1

    def rows(x_ref, o_ref, h_ref):
        @pl.when(j == 0)
        def _():
            h_ref[...] = (_rms(x_ref[...]) * nw_ref[...]).astype(BF16)
            o_ref[...] = jnp.zeros(o_ref.shape, F32)

        h = h_ref[...]
        g = _dot(h, wg_ref[...].astype(BF16))
        u = _dot(h, wu_ref[...].astype(BF16))
        a = (_silu(g) * u).astype(BF16)
        o_ref[...] += _dot(a, wd_ref[...].astype(BF16))

        @pl.when(j == last_j)
        def _():
            y = x_ref[...] + 0.5 * o_ref[...]
            if final_norm:
                y = _rms(y) * fw_ref[...]
            o_ref[...] = y

    rows(xp_ref, op_ref, hp_ref)

    @pl.when(i == pl.num_programs(0) - 1)
    def _():
        rows(xs_ref, os_ref, hs_ref)


def _ffn(xp, xs, norm_w, w_gate, w_up, w_down, final_w, *, final_norm, tm, tf):
    mp, d = xp.shape
    ms = xs.shape[0]
    f = w_gate.shape[1]
    return pl.pallas_call(
        functools.partial(_ffn_kernel, final_norm=final_norm),
        grid=(mp // tm, f // tf),
        in_specs=[
            pl.BlockSpec((tm, d), lambda i, j: (i, 0), pipeline_mode=pl.Buffered(1)),
            pl.BlockSpec((ms, d), lambda i, j: (0, 0), pipeline_mode=pl.Buffered(1)),
            pl.BlockSpec((1, d), lambda i, j: (0, 0)),
            pl.BlockSpec((d, tf), lambda i, j: (0, j)),
            pl.BlockSpec((d, tf), lambda i, j: (0, j)),
            pl.BlockSpec((tf, d), lambda i, j: (j, 0)),
            pl.BlockSpec((1, d), lambda i, j: (0, 0)),
        ],
        out_specs=[pl.BlockSpec((tm, d), lambda i, j: (i, 0)),
                   pl.BlockSpec((ms, d), lambda i, j: (0, 0))],
        out_shape=[jax.ShapeDtypeStruct((mp, d), F32), jax.ShapeDtypeStruct((ms, d), F32)],
        scratch_shapes=[pltpu.VMEM((tm, d), BF16), pltpu.VMEM((ms, d), BF16)],
        compiler_params=_params(),
        name="ffn",
    )(xp, xs, norm_w.reshape(1, d), w_gate, w_up, w_down, final_w.reshape(1, d))


def _inproj_kernel(xp_ref, xs_ref, nw_ref, w_ref, op_ref, os_ref, up_ref, us_ref):
    i = pl.program_id(0)
    j = pl.program_id(1)

    def rows(x_ref, o_ref, u_ref):
        @pl.when(j == 0)
        def _():
            u_ref[...] = (_rms(x_ref[...]) * nw_ref[...]).astype(BF16)

        o_ref[...] = _dot_nt(u_ref[...], w_ref[...])

    rows(xp_ref, op_ref, up_ref)

    @pl.when(i == pl.num_programs(0) - 1)
    def _():
        rows(xs_ref, os_ref, us_ref)


def _inproj(xp, xs, norm_w, wt, *, tm, tn):
    mp, d = xp.shape
    ms = xs.shape[0]
    n = wt.shape[0]
    last_i = mp // tm - 1
    return pl.pallas_call(
        _inproj_kernel,
        grid=(mp // tm, n // tn),
        in_specs=[
            pl.BlockSpec((tm, d), lambda i, j: (i, 0), pipeline_mode=pl.Buffered(1)),
            pl.BlockSpec((ms, d), lambda i, j: (0, 0), pipeline_mode=pl.Buffered(1)),
            pl.BlockSpec((1, d), lambda i, j: (0, 0)),
            pl.BlockSpec((tn, d), lambda i, j: (j, 0)),
        ],
        out_specs=[pl.BlockSpec((tm, tn), lambda i, j: (i, j)),
                   pl.BlockSpec((ms, tn), lambda i, j: (0, jnp.where(i == last_i, j, 0)))],
        out_shape=[jax.ShapeDtypeStruct((mp, n), F32), jax.ShapeDtypeStruct((ms, n), F32)],
        scratch_shapes=[pltpu.VMEM((tm, d), BF16), pltpu.VMEM((ms, d), BF16)],
        compiler_params=_params(),
        name="inproj",
    )(xp, xs, norm_w.reshape(1, d), wt)


def _outproj_kernel(yp_ref, ys_ref, w_ref, xp_ref, xs_ref, op_ref, os_ref):
    w = w_ref[...].astype(BF16)
    op_ref[...] = xp_ref[...] + _dot(yp_ref[...].astype(BF16), w)

    @pl.when(pl.program_id(0) == pl.num_programs(0) - 1)
    def _():
        os_ref[...] = xs_ref[...] + _dot(ys_ref[...].astype(BF16), w)


def _outproj(yp, ys, w, xp, xs, *, tm, tn):
    mp, d = yp.shape
    ms = ys.shape[0]
    n = w.shape[1]
    last_i = mp // tm - 1

    def sample_tile(i, j):
        return (0, jnp.where(i == last_i, j, 0))

    return pl.pallas_call(
        _outproj_kernel,
        grid=(mp // tm, n // tn),
        in_specs=[
            pl.BlockSpec((tm, d), lambda i, j: (i, 0)),
            pl.BlockSpec((ms, d), lambda i, j: (0, 0)),
            pl.BlockSpec((d, tn), lambda i, j: (0, j)),
            pl.BlockSpec((tm, tn), lambda i, j: (i, j)),
            pl.BlockSpec((ms, tn), sample_tile),
        ],
        out_specs=[pl.BlockSpec((tm, tn), lambda i, j: (i, j)),
                   pl.BlockSpec((ms, tn), sample_tile)],
        out_shape=[jax.ShapeDtypeStruct((mp, n), F32), jax.ShapeDtypeStruct((ms, n), F32)],
        compiler_params=_params(),
        name="outproj",
    )(yp, ys, w, xp, xs)


def _seg_sources():
    offs = [0]
    for s in IN_SIZES:
        offs.append(offs[-1] + s)
    q, k, v, ig, fg, og, z, xbc, dtr, gates = offs[:-1]
    assert fg == ig + ML_HEADS and SM_FG == SM_IG + ML_HEADS
    return [(SEG_GATES, gates, 2 * D_MODEL), (SEG_OG, og, D_MODEL), (SEG_Z, z, SSM_WIDTH),
            (SEG_V, v, D_MODEL), (SEG_XS, xbc, SSM_WIDTH), (SEG_Q, q, ML_QK), (SEG_K, k, ML_QK),
            (SEG_BC, xbc + SSM_WIDTH, BC_DIM), (SEG_SMALL + SM_DT, dtr, SSM_HEADS),
            (SEG_SMALL + SM_IG, ig, 2 * ML_HEADS)]


def _regroup_kernel(wt_ref, o_ref):
    cols = wt_ref.shape[1]
    small = []
    for dst, src, width in _seg_sources():
        assert src % SUBLANES == 0 and width % SUBLANES == 0
        if dst < SEG_SMALL:
            o_ref[dst:dst + width, :] = wt_ref[src:src + width, :].astype(BF16)
        else:
            assert dst - SEG_SMALL == sum(p.shape[0] for p in small)
            small.append(wt_ref[src:src + width, :])
    used = sum(p.shape[0] for p in small)
    small.append(jnp.zeros((PROJ_DIM - SEG_SMALL - used, cols), F32))
    o_ref[SEG_SMALL:PROJ_DIM, :] = jnp.concatenate(small, axis=0).astype(BF16)


def _regroup_w_in(wt, *, tc):
    n_in, d = wt.shape
    return pl.pallas_call(
        _regroup_kernel,
        grid=(d // tc,),
        in_specs=[pl.BlockSpec((n_in, tc), lambda i: (0, i))],
        out_specs=pl.BlockSpec((PROJ_DIM, tc), lambda i: (0, i)),
        out_shape=jax.ShapeDtypeStruct((PROJ_DIM, d), BF16),
        compiler_params=_params(1),
        name="regroup",
    )(wt)


def _cast_kernel(x_ref, o_ref):
    o_ref[...] = x_ref[...].astype(o_ref.dtype)


def _cast_bf16(x, *, tr):
    r, c = x.shape
    return pl.pallas_call(
        _cast_kernel,
        grid=(r // tr,),
        in_specs=[pl.BlockSpec((tr, c), lambda i: (i, 0))],
        out_specs=pl.BlockSpec((tr, c), lambda i: (i, 0)),
        out_shape=jax.ShapeDtypeStruct((r, c), BF16),
        compiler_params=_params(1),
        name="cast_bf16",
    )(x)


PAD_TOP = SUBLANES


def _mixer_kernel(gates_ref, og_ref, z_ref, v_ref, xs_ref, q_ref, k_ref, bc_ref, sm_ref,
                  bias_ref, alog_ref, convw_ref, convb_ref, dskip_ref, hnorm_ref, snorm_ref, e1_ref,
                  merged_ref, conv_ref, c_ref, n_ref, m_ref, s_ref,
                  xsp_ref, bcp_ref):
    c = pl.program_id(1)

    @pl.when(c == 0)
    def _():
        c_ref[...] = jnp.zeros(c_ref.shape, F32)
        n_ref[...] = jnp.zeros(n_ref.shape, F32)
        m_ref[...] = jnp.full(m_ref.shape, NEG_INIT, F32)
        s_ref[...] = jnp.zeros(s_ref.shape, F32)
        xsp_ref[0:PAD_TOP, :] = jnp.zeros((PAD_TOP, SSM_WIDTH), F32)
        bcp_ref[0:PAD_TOP, :] = jnp.zeros((PAD_TOP, BC_DIM), F32)

    xsp_ref[PAD_TOP:PAD_TOP + CHUNK, :] = xs_ref[...]
    bcp_ref[PAD_TOP:PAD_TOP + CHUNK, :] = bc_ref[...]

    def conv(p_ref, lo, hi):
        acc = p_ref[PAD_TOP - HIST:PAD_TOP - HIST + CHUNK, :] * convw_ref[0:1, lo:hi]
        for j in range(1, CONV_W):
            acc = acc + p_ref[PAD_TOP - HIST + j:PAD_TOP - HIST + j + CHUNK, :] * convw_ref[j:j + 1, lo:hi]
        return _silu(acc + convb_ref[:, lo:hi])

    xs_act = conv(xsp_ref, 0, SSM_WIDTH)
    bc_act = conv(bcp_ref, SSM_WIDTH, XBC_DIM)
    hist_x = xsp_ref[PAD_TOP + CHUNK - HIST:PAD_TOP + CHUNK, :]
    hist_bc = bcp_ref[PAD_TOP + CHUNK - HIST:PAD_TOP + CHUNK, :]
    conv_ref[0, :, 0:SSM_WIDTH] = hist_x
    conv_ref[0, :, SSM_WIDTH:XBC_DIM] = hist_bc
    xsp_ref[PAD_TOP - HIST:PAD_TOP, :] = hist_x
    bcp_ref[PAD_TOP - HIST:PAD_TOP, :] = hist_bc

    lane = lax.broadcasted_iota(jnp.int32, (CHUNK, LANES), 1)
    row = lax.broadcasted_iota(jnp.int32, (CHUNK, LANES), 0)
    lane1 = lax.broadcasted_iota(jnp.int32, (1, LANES), 1)
    is_dt = lane < SM_IG
    is_li = (lane >= SM_IG) & (lane < SM_FG)
    is_lf = (lane >= SM_FG) & (lane < SM_FG + ML_HEADS)
    pre = sm_ref[...] + bias_ref[...]
    dt = jnp.where(is_dt, _softplus(pre), 0.0)
    lf = jnp.where(is_lf, -_softplus(-pre), 0.0)
    li = jnp.where(is_li, pre, 0.0)
    a_row = jnp.where(lane1 < SM_IG, -jnp.exp(alog_ref[...]), 0.0)
    causal = row >= lane
    tril = jnp.where(causal, 1.0, 0.0).astype(BF16)
    hi, mid, lo = _split3(dt * a_row + lf)
    cs = _dot(tril, hi) + _dot(tril, mid) + _dot(tril, lo)
    cs_t = cs.T
    li_t = li.T
    dt_t = dt.T
    b_last = cs[CHUNK - 1:CHUNK, :]
    ecs = jnp.exp(cs)

    q_all = (q_ref[...] * (ML_DK ** -0.5)).astype(BF16)
    k_all = k_ref[...]
    v_all = v_ref[...].astype(BF16)
    og = og_ref[...]
    m_old = m_ref[0]
    m_new_row = m_old
    y_a_parts = []
    for h in range(ML_HEADS):
        ck = SM_FG + h
        b_col = cs[:, ck:ck + 1]
        b_row = cs_t[ck:ck + 1, :]
        li_row = li_t[SM_IG + h:SM_IG + h + 1, :]
        li_col = li[:, SM_IG + h:SM_IG + h + 1]
        m_prev = m_old[:, h:h + 1]
        dmat = jnp.where(causal, b_col - b_row + li_row, -jnp.inf)
        inter = b_col + m_prev
        m_t = jnp.maximum(inter, jnp.max(dmat, axis=1, keepdims=True))
        w_intra = jnp.exp(dmat - m_t)
        w_inter = jnp.exp(inter - m_t)
        qh = q_all[:, h * ML_DK:(h + 1) * ML_DK]
        kh = k_all[:, h * ML_DK:(h + 1) * ML_DK]
        vh = v_all[:, h * ML_DV:(h + 1) * ML_DV]
        c_old = c_ref[0, h]
        n_old = n_ref[0, h:h + 1, :]
        s = _dot_nt(qh, kh.astype(BF16)) * w_intra
        num = _dot(s.astype(BF16), vh) + w_inter * _dot(qh, c_old.astype(BF16))
        qn = jnp.sum(qh.astype(F32) * n_old, axis=1, keepdims=True)
        den = jnp.sum(s, axis=1, keepdims=True) + w_inter * qn
        hh = num / jnp.maximum(jnp.abs(den), jnp.exp(-m_t))
        m_new = m_t[CHUNK - 1:CHUNK, :]
        bl = b_col[CHUNK - 1:CHUNK, :]
        w_end = jnp.exp(bl - b_col + li_col - m_new)
        decay = jnp.exp(bl + m_prev - m_new)
        kw = kh * w_end
        c_ref[0, h] = decay * c_old + _dot_tn(kw.astype(BF16), vh)
        n_ref[0, h:h + 1, :] = decay * n_old + jnp.sum(kw, axis=0, keepdims=True)
        m_new_row = jnp.where(lane1 == h, m_new, m_new_row)
        hn = _rms(hh) * hnorm_ref[:, h * ML_DV:(h + 1) * ML_DV]
        y_a_parts.append(_sigmoid(og[:, h * ML_DV:(h + 1) * ML_DV]) * hn)
    m_ref[0] = m_new_row
    y_a = jnp.concatenate(y_a_parts, axis=1)

    e1 = e1_ref[...]
    e_b = _expand_heads(jnp.where(is_dt, ecs, 0.0), e1)
    e_w = _expand_heads(jnp.exp(b_last - cs) * dt, e1)
    xs_bf = xs_act.astype(BF16)
    xw_bf = (xs_act * e_w).astype(BF16)
    half = lane < SSM_HEADDIM
    gw = SSM_WIDTH // SSM_GROUPS
    hpg = SSM_HEADS // SSM_GROUPS
    y_parts = []
    for g in range(SSM_GROUPS):
        bg = bc_act[:, g * SSM_STATE:(g + 1) * SSM_STATE].astype(BF16)
        cg = bc_act[:, (SSM_GROUPS + g) * SSM_STATE:(SSM_GROUPS + g + 1) * SSM_STATE].astype(BF16)
        cb = _dot_nt(cg, bg)
        s_old = s_ref[0, g * gw:(g + 1) * gw, :]
        y_inter = _dot_nt(cg, s_old.astype(BF16))
        pairs = []
        for j in range(hpg // 2):
            x_pair = xs_bf[:, g * gw + j * LANES:g * gw + (j + 1) * LANES]
            res = []
            for hh_ in range(2):
                h = g * hpg + 2 * j + hh_
                dec = jnp.exp(jnp.where(causal, cs[:, h:h + 1] - cs_t[h:h + 1, :], -jnp.inf))
                mm = (cb * dec * dt_t[h:h + 1, :]).astype(BF16)
                res.append(_dot(mm, x_pair))
            pairs.append(jnp.where(half, res[0], res[1]))
        y_intra = jnp.concatenate(pairs, axis=1)
        y_parts.append(y_intra + e_b[:, g * gw:(g + 1) * gw] * y_inter)
        upd = _dot_tn(xw_bf[:, g * gw:(g + 1) * gw], bg)
        for hh_ in range(hpg):
            h = g * hpg + hh_
            lo_r = g * gw + hh_ * SSM_HEADDIM
            s_ref[0, lo_r:lo_r + SSM_HEADDIM, :] = (
                ecs[CHUNK - 1:CHUNK, h:h + 1] * s_ref[0, lo_r:lo_r + SSM_HEADDIM, :]
                + upd[hh_ * SSM_HEADDIM:(hh_ + 1) * SSM_HEADDIM, :])
    y = jnp.concatenate(y_parts, axis=1)
    y_s = (y + dskip_ref[...] * xs_act) * _silu(z_ref[...])
    y_b = jnp.concatenate(
        [_rms(y_s[:, g * gw:(g + 1) * gw]) for g in range(SSM_GROUPS)], axis=1) * snorm_ref[...]

    gates = gates_ref[...]
    merged_ref[...] = (_sigmoid(gates[:, 0:D_MODEL]) * y_a
                       + _sigmoid(gates[:, D_MODEL:2 * D_MODEL]) * y_b).astype(merged_ref.dtype)


def _const_spec(shape, single_buffer=False):
    kwargs = dict(pipeline_mode=pl.Buffered(1)) if single_buffer else {}
    return pl.BlockSpec(shape, lambda *_: (0,) * len(shape), **kwargs)


def _mixer_params(w):
    return [w["bias_row"], w["alog_row"], w["ssm_conv_w"], w["ssm_conv_b"].reshape(1, XBC_DIM),
            w["dskip_row"], w["ml_head_norm"].reshape(1, D_MODEL), w["ssm_norm"].reshape(1, SSM_WIDTH),
            w["e1"]]


_MIXER_PARAM_SHAPES = [(1, LANES), (1, LANES), (CONV_W, XBC_DIM), (1, XBC_DIM), (1, SSM_WIDTH),
                       (1, D_MODEL), (1, SSM_WIDTH), (LANES, SSM_WIDTH)]


def _mixer_chunk(proj, w, *, n_seq, n_chunks):
    def seg(width, start):
        assert start % width == 0
        return pl.BlockSpec((CHUNK, width), lambda b, c: (b * n_chunks + c, start // width))

    def per_seq(shape):
        return pl.BlockSpec((1,) + shape, lambda b, c: (b,) + (0,) * len(shape))

    state_shapes = [(HIST, XBC_DIM), (ML_HEADS, ML_DK, ML_DV), (ML_HEADS, ML_DK), (1, LANES),
                    (SSM_WIDTH, SSM_STATE)]
    in_specs = [
        seg(2 * D_MODEL, SEG_GATES), seg(D_MODEL, SEG_OG), seg(SSM_WIDTH, SEG_Z), seg(D_MODEL, SEG_V),
        seg(SSM_WIDTH, SEG_XS), seg(ML_QK, SEG_Q), seg(ML_QK, SEG_K), seg(BC_DIM, SEG_BC),
        seg(LANES, SEG_SMALL),
    ] + [_const_spec(s) for s in _MIXER_PARAM_SHAPES]
    out_specs = [pl.BlockSpec((CHUNK, D_MODEL), lambda b, c: (b * n_chunks + c, 0))] + [
        per_seq(s) for s in state_shapes]
    out_shape = [jax.ShapeDtypeStruct((proj.shape[0], D_MODEL), BF16)] + [
        jax.ShapeDtypeStruct((n_seq,) + s, F32) for s in state_shapes]
    return pl.pallas_call(
        _mixer_kernel,
        grid=(n_seq, n_chunks),
        in_specs=in_specs,
        out_specs=out_specs,
        out_shape=out_shape,
        scratch_shapes=[pltpu.VMEM((PAD_TOP + CHUNK, SSM_WIDTH), F32),
                        pltpu.VMEM((PAD_TOP + CHUNK, BC_DIM), F32)],
        compiler_params=_params(),
        name="mixer_chunk",
    )(*([proj] * 9), *_mixer_params(w))


def _step_kernel(proj_ref, conv0_ref, n0_ref, m0_ref, c0_ref, s0_ref,
                 bias_ref, alog_ref, convw_ref, convb_ref, dskip_ref, hnorm_ref, snorm_ref, e1_ref,
                 merged_ref, conv_ref, n_ref, m_ref, c_ref, s_ref,
                 qf_ref, kt_ref, vb_ref, cmf_ref, bb_ref, xwt_ref, dec_ref, accq_ref, accs_ref):
    b = pl.program_id(0)
    ns = proj_ref.shape[0]
    gw = SSM_WIDTH // SSM_GROUPS
    hpg = SSM_HEADS // SSM_GROUPS
    lane = lax.broadcasted_iota(jnp.int32, (ns, LANES), 1)
    lane1 = lax.broadcasted_iota(jnp.int32, (1, LANES), 1)
    is_dt = lane < SM_IG
    is_li = (lane >= SM_IG) & (lane < SM_FG)
    is_lf = (lane >= SM_FG) & (lane < SM_FG + ML_HEADS)

    def gate_scalars():
        pre = proj_ref[:, SEG_SMALL:SEG_SMALL + LANES] + bias_ref[...]
        dt = jnp.where(is_dt, _softplus(pre), 0.0)
        lf = pltpu.roll(jnp.where(is_lf, -_softplus(-pre), 0.0), LANES - ML_HEADS, axis=1)
        li = jnp.where(is_li, pre, 0.0)
        inter = lf + m0_ref[...]
        m_t = jnp.maximum(inter, li)
        a_row = jnp.where(lane1 < SM_IG, -jnp.exp(alog_ref[...]), 0.0)
        return dict(dt=dt, m_t=m_t, w_intra=jnp.exp(li - m_t), w_inter=jnp.exp(inter - m_t),
                    e=jnp.exp(dt * a_row))

    def conv_act(lo, hi, seg, store):
        x_new = proj_ref[:, seg:seg + hi - lo]
        st = [conv0_ref[:, j * XBC_DIM + lo:j * XBC_DIM + hi] for j in range(HIST)]
        acc = st[0] * convw_ref[0:1, lo:hi]
        for j in range(1, HIST):
            acc = acc + st[j] * convw_ref[j:j + 1, lo:hi]
        acc = acc + x_new * convw_ref[HIST:HIST + 1, lo:hi]
        if store:
            for j in range(1, HIST):
                conv_ref[:, (j - 1) * XBC_DIM + lo:(j - 1) * XBC_DIM + hi] = st[j]
            conv_ref[:, (HIST - 1) * XBC_DIM + lo:(HIST - 1) * XBC_DIM + hi] = x_new
        return _silu(acc + convb_ref[:, lo:hi])

    def head_col(x, h):
        return x[:, SM_IG + h:SM_IG + h + 1]

    @pl.when(b == 0)
    def _():
        gs = gate_scalars()
        m_ref[...] = gs["m_t"]
        dec_ref[...] = jnp.where(is_dt, gs["e"], gs["w_inter"])
        qf_ref[...] = (proj_ref[:, SEG_Q:SEG_Q + ML_QK] * (ML_DK ** -0.5)).astype(BF16).astype(F32)
        vb_ref[...] = proj_ref[:, SEG_V:SEG_V + D_MODEL].astype(BF16)
        for h in range(ML_HEADS):
            sl = slice(h * ML_DK, (h + 1) * ML_DK)
            kw = proj_ref[:, SEG_K + h * ML_DK:SEG_K + (h + 1) * ML_DK] * head_col(gs["w_intra"], h)
            n_ref[:, sl] = head_col(gs["w_inter"], h) * n0_ref[:, sl] + kw
            kt_ref[sl, :] = kw.T
        xs_act = conv_act(0, SSM_WIDTH, SEG_XS, True)
        bc_act = conv_act(SSM_WIDTH, XBC_DIM, SEG_BC, True)
        bb_ref[...] = bc_act[:, 0:SSM_GROUPS * SSM_STATE].astype(BF16)
        cmf_ref[...] = bc_act[:, SSM_GROUPS * SSM_STATE:BC_DIM]
        xw = xs_act * _expand_heads(gs["dt"], e1_ref[...])
        for g in range(SSM_GROUPS):
            xwt_ref[g * gw:(g + 1) * gw, :] = xw[:, g * gw:(g + 1) * gw].T
        accq_ref[...] = jnp.zeros(accq_ref.shape, F32)
        accs_ref[...] = jnp.zeros(accs_ref.shape, F32)

    base = pl.multiple_of((b // SUBLANES) * SUBLANES, SUBLANES)
    is_row = lax.broadcasted_iota(jnp.int32, (SUBLANES, 1), 0) == b % SUBLANES
    is_seq = lax.broadcasted_iota(jnp.int32, (1, ns), 1) == b
    q8 = jnp.where(is_row, qf_ref[pl.ds(base, SUBLANES), :], 0.0).astype(BF16)
    c8 = jnp.where(is_row, cmf_ref[pl.ds(base, SUBLANES), :], 0.0).astype(BF16)
    drow = dec_ref[pl.ds(b, 1), :]
    for h in range(ML_HEADS):
        c_old = c0_ref[0, h]
        accq_ref[pl.ds(base, SUBLANES), h * ML_DV:(h + 1) * ML_DV] += _dot(
            q8[:, h * ML_DK:(h + 1) * ML_DK], c_old.astype(BF16))
        kt = jnp.where(is_seq, kt_ref[h * ML_DK:(h + 1) * ML_DK, :], 0.0).astype(BF16)
        c_ref[0, h] = (drow[:, SM_IG + h:SM_IG + h + 1] * c_old
                       + _dot(kt, vb_ref[:, h * ML_DV:(h + 1) * ML_DV]))
    for g in range(SSM_GROUPS):
        s_old = s0_ref[0, g * gw:(g + 1) * gw, :]
        accs_ref[pl.ds(base, SUBLANES), g * gw:(g + 1) * gw] += _dot_nt(
            c8[:, g * SSM_STATE:(g + 1) * SSM_STATE], s_old.astype(BF16))
        xwt = jnp.where(is_seq, xwt_ref[g * gw:(g + 1) * gw, :], 0.0).astype(BF16)
        upd = _dot(xwt, bb_ref[:, g * SSM_STATE:(g + 1) * SSM_STATE])
        for hh_ in range(hpg):
            h = g * hpg + hh_
            rs = slice(hh_ * SSM_HEADDIM, (hh_ + 1) * SSM_HEADDIM)
            s_ref[0, g * gw + hh_ * SSM_HEADDIM:g * gw + (hh_ + 1) * SSM_HEADDIM, :] = (
                drow[:, h:h + 1] * s_old[rs, :] + upd[rs, :])

    @pl.when(b == pl.num_programs(0) - 1)
    def _():
        gs = gate_scalars()
        q = qf_ref[...]
        y_a_parts = []
        for h in range(ML_HEADS):
            qh = q[:, h * ML_DK:(h + 1) * ML_DK]
            kh = proj_ref[:, SEG_K + h * ML_DK:SEG_K + (h + 1) * ML_DK].astype(BF16).astype(F32)
            vh = vb_ref[:, h * ML_DV:(h + 1) * ML_DV].astype(F32)
            w_inter = head_col(gs["w_inter"], h)
            s = jnp.sum(qh * kh, axis=1, keepdims=True) * head_col(gs["w_intra"], h)
            qn = jnp.sum(qh * n0_ref[:, h * ML_DK:(h + 1) * ML_DK], axis=1, keepdims=True)
            num = s.astype(BF16).astype(F32) * vh + w_inter * accq_ref[:, h * ML_DV:(h + 1) * ML_DV]
            den = s + w_inter * qn
            hh = num / jnp.maximum(jnp.abs(den), jnp.exp(-head_col(gs["m_t"], h)))
            hn = _rms(hh) * hnorm_ref[:, h * ML_DV:(h + 1) * ML_DV]
            y_a_parts.append(_sigmoid(proj_ref[:, SEG_OG + h * ML_DV:SEG_OG + (h + 1) * ML_DV]) * hn)
        y_a = jnp.concatenate(y_a_parts, axis=1)

        xs_act = conv_act(0, SSM_WIDTH, SEG_XS, False)
        bb = bb_ref[...].astype(F32)
        cm = cmf_ref[...].astype(BF16).astype(F32)
        e1 = e1_ref[...]
        x_dt = xs_act * _expand_heads(gs["dt"], e1)
        e_e = _expand_heads(jnp.where(is_dt, gs["e"], 0.0), e1)
        y_parts = []
        for g in range(SSM_GROUPS):
            gl = slice(g * SSM_STATE, (g + 1) * SSM_STATE)
            cb = jnp.sum(cm[:, gl] * bb[:, gl], axis=1, keepdims=True)
            cl = slice(g * gw, (g + 1) * gw)
            y_parts.append(cb * x_dt[:, cl] + e_e[:, cl] * accs_ref[:, cl])
        y = jnp.concatenate(y_parts, axis=1)
        y_s = (y + dskip_ref[...] * xs_act) * _silu(proj_ref[:, SEG_Z:SEG_Z + SSM_WIDTH])
        y_b = jnp.concatenate(
            [_rms(y_s[:, g * gw:(g + 1) * gw]) for g in range(SSM_GROUPS)], axis=1) * snorm_ref[...]
        merged_ref[...] = (_sigmoid(proj_ref[:, SEG_GATES:SEG_GATES + D_MODEL]) * y_a
                           + _sigmoid(proj_ref[:, SEG_GATES + D_MODEL:SEG_GATES + 2 * D_MODEL]) * y_b)


def _mixer_step(proj, conv0, n0, m0, c0, s0, w):
    ns = proj.shape[0]

    def per_seq(shape):
        return pl.BlockSpec((1,) + shape, lambda b: (b,) + (0,) * len(shape))

    row_shapes = [(ns, HIST * XBC_DIM), (ns, ML_QK), (ns, LANES)]
    mat_shapes = [(ML_HEADS, ML_DK, ML_DV), (SSM_WIDTH, SSM_STATE)]
    in_specs = ([_const_spec((ns, PROJ_DIM), True)] + [_const_spec(s, True) for s in row_shapes]
                + [per_seq(s) for s in mat_shapes] + [_const_spec(s) for s in _MIXER_PARAM_SHAPES])
    out_specs = ([_const_spec((ns, D_MODEL))] + [_const_spec(s) for s in row_shapes]
                 + [per_seq(s) for s in mat_shapes])
    out_shape = ([jax.ShapeDtypeStruct((ns, D_MODEL), F32)]
                 + [jax.ShapeDtypeStruct(s, F32) for s in row_shapes]
                 + [jax.ShapeDtypeStruct((ns,) + s, F32) for s in mat_shapes])
    scratch = [
        pltpu.VMEM((ns, ML_QK), F32),
        pltpu.VMEM((ML_QK, ns), F32),
        pltpu.VMEM((ns, D_MODEL), BF16),
        pltpu.VMEM((ns, SSM_GROUPS * SSM_STATE), F32),
        pltpu.VMEM((ns, SSM_GROUPS * SSM_STATE), BF16),
        pltpu.VMEM((SSM_WIDTH, ns), F32),
        pltpu.VMEM((ns, LANES), F32),
        pltpu.VMEM((ns, D_MODEL), F32),
        pltpu.VMEM((ns, SSM_WIDTH), F32),
    ]
    return pl.pallas_call(
        _step_kernel,
        grid=(ns,),
        in_specs=in_specs,
        out_specs=out_specs,
        out_shape=out_shape,
        scratch_shapes=scratch,
        compiler_params=_params(1),
        name="mixer_step",
    )(proj, conv0, n0, m0, c0, s0, *_mixer_params(w))


def _pad_lanes(vec, offset):
    pad = [(0, 0)] * (vec.ndim - 1) + [(offset, LANES - offset - vec.shape[-1])]
    return jnp.pad(vec.astype(F32), pad)


def kernel(x_prompt, x_sample, state_conv, state_mlstm_C, state_mlstm_n, state_mlstm_m, state_ssm, ffn1_norm, ffn1_w_gate, ffn1_w_up, ffn1_w_down, mix_norm, w_in, ml_i_bias, ml_f_bias, ml_head_norm, ssm_conv_w, ssm_conv_b, ssm_dt_bias, ssm_A_log, ssm_D, ssm_norm, w_out, ffn2_norm, ffn2_w_gate, ffn2_w_up, ffn2_w_down, final_norm):
    depth = w_in.shape[0]
    assert depth == 1, "the final norm is fused into the (single) layer's second FFN"
    bp, seq, d = x_prompt.shape
    bs = x_sample.shape[0]
    assert x_sample.shape[1] == 1 and seq % CHUNK == 0 and bs % SUBLANES == 0
    l = 0
    tm = 1024

    head_of_lane = jnp.arange(SSM_WIDTH) // SSM_HEADDIM
    w = {
        "bias_row": (_pad_lanes(ssm_dt_bias[l], SM_DT) + _pad_lanes(ml_i_bias[l], SM_IG)
                     + _pad_lanes(ml_f_bias[l], SM_FG)).reshape(1, LANES),
        "alog_row": _pad_lanes(ssm_A_log[l], SM_DT).reshape(1, LANES),
        "ml_head_norm": ml_head_norm[l], "ssm_conv_w": ssm_conv_w[l], "ssm_conv_b": ssm_conv_b[l],
        "dskip_row": jnp.repeat(ssm_D[l].astype(F32), SSM_HEADDIM).reshape(1, SSM_WIDTH),
        "ssm_norm": ssm_norm[l],
        "e1": (jnp.arange(LANES)[:, None] == head_of_lane[None, :]).astype(BF16),
    }

    xp = x_prompt.reshape(bp * seq, d)
    xs = x_sample.reshape(bs, d)
    xp1, xs1 = _ffn(xp, xs, ffn1_norm[l], ffn1_w_gate[l], ffn1_w_up[l], ffn1_w_down[l], final_norm,
                    final_norm=False, tm=tm, tf=256)
    w_in_r = _regroup_w_in(jnp.swapaxes(w_in[l], 0, 1), tc=256)
    proj_p, proj_s = _inproj(xp1, xs1, mix_norm[l], w_in_r, tm=tm, tn=PROJ_TILE)

    mp, p_conv, p_c, p_n, p_m, p_s = _mixer_chunk(proj_p, w, n_seq=bp, n_chunks=seq // CHUNK)
    ms, s_conv, s_n, s_m, s_c, s_s = _mixer_step(
        proj_s, state_conv[l].reshape(bs, HIST * XBC_DIM), state_mlstm_n[l].reshape(bs, ML_QK),
        _pad_lanes(state_mlstm_m[l], SM_IG), state_mlstm_C[l],
        state_ssm[l].reshape(bs, SSM_WIDTH, SSM_STATE), w)

    xp2, xs2 = _outproj(mp, ms, w_out[l], xp1, xs1, tm=tm, tn=512)
    yp, ys = _ffn(xp2, xs2, ffn2_norm[l], _cast_bf16(ffn2_w_gate[l], tr=256),
                  _cast_bf16(ffn2_w_up[l], tr=256), _cast_bf16(ffn2_w_down[l], tr=512), final_norm,
                  final_norm=True, tm=tm, tf=512)

    ssm_shape = (SSM_HEADS, SSM_HEADDIM, SSM_STATE)
    return (yp.reshape(bp, seq, d), ys.reshape(bs, 1, d),
            p_conv[None], p_c[None], p_n[None], p_m[:, 0, :ML_HEADS][None], p_s.reshape(bp, *ssm_shape)[None],
            s_conv.reshape(bs, HIST, XBC_DIM)[None], s_c[None], s_n.reshape(bs, ML_HEADS, ML_DK)[None],
            s_m[:, SM_IG:SM_IG + ML_HEADS][None], s_s.reshape(bs, *ssm_shape)[None])
```

```python
import functools

import jax
import jax.numpy as jnp
from jax import lax
from jax.experimental import pallas as pl
from jax.experimental.pallas import tpu as pltpu

F32 = jnp.float32
BF16 = jnp.bfloat16

EPS = 1e-6
NEG_INIT = -1e30

D_MODEL = 2048
ML_HEADS = 4
ML_DV = 512
ML_DK = 256
ML_QK = ML_HEADS * ML_DK
SSM_HEADS = 32
SSM_HEADDIM = 64
SSM_GROUPS = 2
SSM_STATE = 128
SSM_WIDTH = SSM_HEADS * SSM_HEADDIM
CONV_W = 4
HIST = CONV_W - 1
BC_DIM = 2 * SSM_GROUPS * SSM_STATE
XBC_DIM = SSM_WIDTH + BC_DIM
CHUNK = 128
IN_SIZES = (ML_QK, ML_QK, D_MODEL, ML_HEADS, ML_HEADS, D_MODEL, SSM_WIDTH, XBC_DIM, SSM_HEADS, 2 * D_MODEL)

LANES = 128
SUBLANES = 8

SEG_GATES = 0
SEG_OG = SEG_GATES + 2 * D_MODEL
SEG_Z = SEG_OG + D_MODEL
SEG_V = SEG_Z + SSM_WIDTH
SEG_XS = SEG_V + D_MODEL
SEG_Q = SEG_XS + SSM_WIDTH
SEG_K = SEG_Q + ML_QK
SEG_BC = SEG_K + ML_QK
SEG_SMALL = SEG_BC + BC_DIM
PROJ_TILE = 1536
PROJ_DIM = -(-(SEG_SMALL + LANES) // PROJ_TILE) * PROJ_TILE
SM_DT = 0
SM_IG = SM_DT + SSM_HEADS
SM_FG = SM_IG + ML_HEADS

VMEM_LIMIT = 60 * 1024 * 1024


def _rms(x):
    return x * lax.rsqrt(jnp.mean(x * x, axis=-1, keepdims=True) + EPS)


def _sigmoid(x):
    return 0.5 * jnp.tanh(0.5 * x) + 0.5


def _silu(x):
    h = 0.5 * x
    return h * jnp.tanh(h) + h


def _softplus(x):
    return jnp.maximum(x, 0.0) + jnp.log(1.0 + jnp.exp(-jnp.abs(x)))


def _dot(a, b):
    return jnp.dot(a, b, preferred_element_type=F32)


def _dot_nt(a, b):
    return lax.dot_general(a, b, (((1,), (1,)), ((), ())), preferred_element_type=F32)


def _dot_tn(a, b):
    return lax.dot_general(a, b, (((0,), (0,)), ((), ())), preferred_element_type=F32)


def _split3(x):
    hi = x.astype(BF16)
    r1 = x - hi.astype(F32)
    mid = r1.astype(BF16)
    lo = (r1 - mid.astype(F32)).astype(BF16)
    return hi, mid, lo


def _expand_heads(x, e1):
    x_hi, x_mid, x_lo = _split3(x)
    return _dot(x_hi, e1) + _dot(x_mid, e1) + _dot(x_lo, e1)


def _params(dims=2):
    return pltpu.CompilerParams(dimension_semantics=("arbitrary",) * dims, vmem_limit_bytes=VMEM_LIMIT)


def _ffn_kernel(xp_ref, xs_ref, nw_ref, wg_ref, wu_ref, wd_ref, fw_ref, op_ref, os_ref, hp_ref, hs_ref,
                *, final_norm):
    i = pl.program_id(0)
    j = pl.program_id(1)
    last_j = pl.num_programs(1) - 1

    def rows(x_ref, o_ref, h_ref):
        @pl.when(j == 0)
        def _():
            h_ref[...] = (_rms(x_ref[...]) * nw_ref[...]).astype(BF16)
            o_ref[...] = jnp.zeros(o_ref.shape, F32)

        h = h_ref[...]
        g = _dot(h, wg_ref[...].astype(BF16))
        u = _dot(h, wu_ref[...].astype(BF16))
        a = (_silu(g) * u).astype(BF16)
        o_ref[...] += _dot(a, wd_ref[...].astype(BF16))

        @pl.when(j == last_j)
        def _():
            y = x_ref[...] + 0.5 * o_ref[...]
            if final_norm:
                y = _rms(y) * fw_ref[...]
            o_ref[...] = y

    rows(xp_ref, op_ref, hp_ref)

    @pl.when(i == pl.num_programs(0) - 1)
    def _():
        rows(xs_ref, os_ref, hs_ref)


def _ffn(xp, xs, norm_w, w_gate, w_up, w_down, final_w, *, final_norm, tm, tf):
    mp, d = xp.shape
    ms = xs.shape[0]
    f = w_gate.shape[1]
    return pl.pallas_call(
        functools.partial(_ffn_kernel, final_norm=final_norm),
        grid=(mp // tm, f // tf),
        in_specs=[
            pl.BlockSpec((tm, d), lambda i, j: (i, 0), pipeline_mode=pl.Buffered(1)),
            pl.BlockSpec((ms, d), lambda i, j: (0, 0), pipeline_mode=pl.Buffered(1)),
            pl.BlockSpec((1, d), lambda i, j: (0, 0)),
            pl.BlockSpec((d, tf), lambda i, j: (0, j)),
            pl.BlockSpec((d, tf), lambda i, j: (0, j)),
            pl.BlockSpec((tf, d), lambda i, j: (j, 0)),
            pl.BlockSpec((1, d), lambda i, j: (0, 0)),
        ],
        out_specs=[pl.BlockSpec((tm, d), lambda i, j: (i, 0)),
                   pl.BlockSpec((ms, d), lambda i, j: (0, 0))],
        out_shape=[jax.ShapeDtypeStruct((mp, d), F32), jax.ShapeDtypeStruct((ms, d), F32)],
        scratch_shapes=[pltpu.VMEM((tm, d), BF16), pltpu.VMEM((ms, d), BF16)],
        compiler_params=_params(),
        name="ffn",
    )(xp, xs, norm_w.reshape(1, d), w_gate, w_up, w_down, final_w.reshape(1, d))


def _inproj_kernel(xp_ref, xs_ref, nw_ref, w_ref, op_ref, os_ref, up_ref, us_ref):
    i = pl.program_id(0)
    j = pl.program_id(1)

    def rows(x_ref, o_ref, u_ref):
        @pl.when(j == 0)
        def _():
            u_ref[...] = (_rms(x_ref[...]) * nw_ref[...]).astype(BF16)

        o_ref[...] = _dot_nt(u_ref[...], w_ref[...])

    rows(xp_ref, op_ref, up_ref)

    @pl.when(i == pl.num_programs(0) - 1)
    def _():
        rows(xs_ref, os_ref, us_ref)


def _inproj(xp, xs, norm_w, wt, *, tm, tn):
    mp, d = xp.shape
    ms = xs.shape[0]
    n = wt.shape[0]
    last_i = mp // tm - 1
    return pl.pallas_call(
        _inproj_kernel,
        grid=(mp // tm, n // tn),
        in_specs=[
            pl.BlockSpec((tm, d), lambda i, j: (i, 0), pipeline_mode=pl.Buffered(1)),
            pl.BlockSpec((ms, d), lambda i, j: (0, 0), pipeline_mode=pl.Buffered(1)),
            pl.BlockSpec((1, d), lambda i, j: (0, 0)),
            pl.BlockSpec((tn, d), lambda i, j: (j, 0)),
        ],
        out_specs=[pl.BlockSpec((tm, tn), lambda i, j: (i, j)),
                   pl.BlockSpec((ms, tn), lambda i, j: (0, jnp.where(i == last_i, j, 0)))],
        out_shape=[jax.ShapeDtypeStruct((mp, n), F32), jax.ShapeDtypeStruct((ms, n), F32)],
        scratch_shapes=[pltpu.VMEM((tm, d), BF16), pltpu.VMEM((ms, d), BF16)],
        compiler_params=_params(),
        name="inproj",
    )(xp, xs, norm_w.reshape(1, d), wt)


def _outproj_kernel(yp_ref, ys_ref, w_ref, xp_ref, xs_ref, op_ref, os_ref):
    w = w_ref[...].astype(BF16)
    op_ref[...] = xp_ref[...] + _dot(yp_ref[...].astype(BF16), w)

    @pl.when(pl.program_id(0) == pl.num_programs(0) - 1)
    def _():
        os_ref[...] = xs_ref[...] + _dot(ys_ref[...].astype(BF16), w)


def _outproj(yp, ys, w, xp, xs, *, tm, tn):
    mp, d = yp.shape
    ms = ys.shape[0]
    n = w.shape[1]
    last_i = mp // tm - 1

    def sample_tile(i, j):
        return (0, jnp.where(i == last_i, j, 0))

    return pl.pallas_call(
        _outproj_kernel,
        grid=(mp // tm, n // tn),
        in_specs=[
            pl.BlockSpec((tm, d), lambda i, j: (i, 0)),
            pl.BlockSpec((ms, d), lambda i, j: (0, 0)),
            pl.BlockSpec((d, tn), lambda i, j: (0, j)),
            pl.BlockSpec((tm, tn), lambda i, j: (i, j)),
            pl.BlockSpec((ms, tn), sample_tile),
        ],
        out_specs=[pl.BlockSpec((tm, tn), lambda i, j: (i, j)),
                   pl.BlockSpec((ms, tn), sample_tile)],
        out_shape=[jax.ShapeDtypeStruct((mp, n), F32), jax.ShapeDtypeStruct((ms, n), F32)],
        compiler_params=_params(),
        name="outproj",
    )(yp, ys, w, xp, xs)


def _seg_sources():
    offs = [0]
    for s in IN_SIZES:
        offs.append(offs[-1] + s)
    q, k, v, ig, fg, og, z, xbc, dtr, gates = offs[:-1]
    assert fg == ig + ML_HEADS and SM_FG == SM_IG + ML_HEADS
    return [(SEG_GATES, gates, 2 * D_MODEL), (SEG_OG, og, D_MODEL), (SEG_Z, z, SSM_WIDTH),
            (SEG_V, v, D_MODEL), (SEG_XS, xbc, SSM_WIDTH), (SEG_Q, q, ML_QK), (SEG_K, k, ML_QK),
            (SEG_BC, xbc + SSM_WIDTH, BC_DIM), (SEG_SMALL + SM_DT, dtr, SSM_HEADS),
            (SEG_SMALL + SM_IG, ig, 2 * ML_HEADS)]


def _regroup_kernel(wt_ref, o_ref):
    cols = wt_ref.shape[1]
    small = []
    for dst, src, width in _seg_sources():
        assert src % SUBLANES == 0 and width % SUBLANES == 0
        if dst < SEG_SMALL:
            o_ref[dst:dst + width, :] = wt_ref[src:src + width, :].astype(BF16)
        else:
            assert dst - SEG_SMALL == sum(p.shape[0] for p in small)
            small.append(wt_ref[src:src + width, :])
    used = sum(p.shape[0] for p in small)
    small.append(jnp.zeros((PROJ_DIM - SEG_SMALL - used, cols), F32))
    o_ref[SEG_SMALL:PROJ_DIM, :] = jnp.concatenate(small, axis=0).astype(BF16)


def _regroup_w_in(wt, *, tc):
    n_in, d = wt.shape
    return pl.pallas_call(
        _regroup_kernel,
        grid=(d // tc,),
        in_specs=[pl.BlockSpec((n_in, tc), lambda i: (0, i))],
        out_specs=pl.BlockSpec((PROJ_DIM, tc), lambda i: (0, i)),
        out_shape=jax.ShapeDtypeStruct((PROJ_DIM, d), BF16),
        compiler_params=_params(1),
        name="regroup",
    )(wt)


def _mixer_kernel(gates_ref, og_ref, z_ref, v_ref, xs_ref, q_ref, k_ref, bc_ref, sm_ref,
                  bias_ref, alog_ref, convw_ref, convb_ref, dskip_ref, hnorm_ref, snorm_ref, e1_ref,
                  merged_ref, conv_ref, c_ref, n_ref, m_ref, s_ref,
                  xh_ref, bch_ref):
    c = pl.program_id(1)

    @pl.when(c == 0)
    def _():
        c_ref[...] = jnp.zeros(c_ref.shape, F32)
        n_ref[...] = jnp.zeros(n_ref.shape, F32)
        m_ref[...] = jnp.full(m_ref.shape, NEG_INIT, F32)
        s_ref[...] = jnp.zeros(s_ref.shape, F32)
        xh_ref[...] = jnp.zeros(xh_ref.shape, F32)
        bch_ref[...] = jnp.zeros(bch_ref.shape, F32)

    row8 = lax.broadcasted_iota(jnp.int32, (SUBLANES, 1), 0)

    def conv(x_ref, h_ref, lo, hi):
        cur = x_ref[...]
        prev = h_ref[...]
        acc = cur * convw_ref[HIST:HIST + 1, lo:hi] + convb_ref[:, lo:hi]
        for back in range(1, CONV_W):
            sh = pltpu.roll(cur, back, axis=0)
            head = jnp.where(row8 < back, pltpu.roll(prev, back, axis=0), sh[0:SUBLANES])
            sh = jnp.concatenate([head, sh[SUBLANES:]], axis=0)
            acc = acc + sh * convw_ref[HIST - back:HIST - back + 1, lo:hi]
        h_ref[...] = cur[CHUNK - SUBLANES:CHUNK]
        conv_ref[0, :, lo:hi] = cur[CHUNK - HIST:CHUNK]
        return _silu(acc)

    xs_act = conv(xs_ref, xh_ref, 0, SSM_WIDTH)
    bc_act = conv(bc_ref, bch_ref, SSM_WIDTH, XBC_DIM)

    lane = lax.broadcasted_iota(jnp.int32, (CHUNK, LANES), 1)
    row = lax.broadcasted_iota(jnp.int32, (CHUNK, LANES), 0)
    lane1 = lax.broadcasted_iota(jnp.int32, (1, LANES), 1)
    is_dt = lane < SM_IG
    is_li = (lane >= SM_IG) & (lane < SM_FG)
    is_lf = (lane >= SM_FG) & (lane < SM_FG + ML_HEADS)
    pre = sm_ref[...] + bias_ref[...]
    dt = jnp.where(is_dt, _softplus(pre), 0.0)
    lf = jnp.where(is_lf, -_softplus(-pre), 0.0)
    li = jnp.where(is_li, pre, 0.0)
    a_row = jnp.where(lane1 < SM_IG, -jnp.exp(alog_ref[...]), 0.0)
    causal = row >= lane
    tril = jnp.where(causal, 1.0, 0.0).astype(BF16)
    hi, mid, lo = _split3(dt * a_row + lf)
    cs = _dot(tril, hi) + _dot(tril, mid) + _dot(tril, lo)
    cs_t = cs.T
    li_t = li.T
    dt_t = dt.T
    b_last = cs[CHUNK - 1:CHUNK, :]
    ecs = jnp.exp(cs)

    q_all = (q_ref[...] * (ML_DK ** -0.5)).astype(BF16)
    k_all = k_ref[...]
    v_all = v_ref[...].astype(BF16)
    og = og_ref[...]
    m_old = m_ref[0]
    m_new_row = m_old
    y_a_parts = []
    for h in range(ML_HEADS):
        ck = SM_FG + h
        b_col = cs[:, ck:ck + 1]
        b_row = cs_t[ck:ck + 1, :]
        li_row = li_t[SM_IG + h:SM_IG + h + 1, :]
        li_col = li[:, SM_IG + h:SM_IG + h + 1]
        m_prev = m_old[:, h:h + 1]
        dmat = jnp.where(causal, b_col - b_row + li_row, -jnp.inf)
        inter = b_col + m_prev
        m_t = jnp.maximum(inter, jnp.max(dmat, axis=1, keepdims=True))
        w_intra = jnp.exp(dmat - m_t)
        w_inter = jnp.exp(inter - m_t)
        qh = q_all[:, h * ML_DK:(h + 1) * ML_DK]
        kh = k_all[:, h * ML_DK:(h + 1) * ML_DK]
        vh = v_all[:, h * ML_DV:(h + 1) * ML_DV]
        c_old = c_ref[0, h]
        n_old = n_ref[0, h:h + 1, :]
        s = _dot_nt(qh, kh.astype(BF16)) * w_intra
        num = _dot(s.astype(BF16), vh) + w_inter * _dot(qh, c_old.astype(BF16))
        qn = jnp.sum(qh.astype(F32) * n_old, axis=1, keepdims=True)
        den = jnp.sum(s, axis=1, keepdims=True) + w_inter * qn
        hh = num / jnp.maximum(jnp.abs(den), jnp.exp(-m_t))
        m_new = m_t[CHUNK - 1:CHUNK, :]
        bl = b_col[CHUNK - 1:CHUNK, :]
        w_end = jnp.exp(bl - b_col + li_col - m_new)
        decay = jnp.exp(bl + m_prev - m_new)
        kw = kh * w_end
        c_ref[0, h] = decay * c_old + _dot_tn(kw.astype(BF16), vh)
        n_ref[0, h:h + 1, :] = decay * n_old + jnp.sum(kw, axis=0, keepdims=True)
        m_new_row = jnp.where(lane1 == h, m_new, m_new_row)
        hn = _rms(hh) * hnorm_ref[:, h * ML_DV:(h + 1) * ML_DV]
        y_a_parts.append(_sigmoid(og[:, h * ML_DV:(h + 1) * ML_DV]) * hn)
    m_ref[0] = m_new_row
    y_a = jnp.concatenate(y_a_parts, axis=1)

    e1 = e1_ref[...]
    e_b = _expand_heads(jnp.where(is_dt, ecs, 0.0), e1)
    e_w = _expand_heads(jnp.exp(b_last - cs) * dt, e1)
    xs_bf = xs_act.astype(BF16)
    xw_bf = (xs_act * e_w).astype(BF16)
    half = lane < SSM_HEADDIM
    gw = SSM_WIDTH // SSM_GROUPS
    hpg = SSM_HEADS // SSM_GROUPS
    y_parts = []
    for g in range(SSM_GROUPS):
        bg = bc_act[:, g * SSM_STATE:(g + 1) * SSM_STATE].astype(BF16)
        cg = bc_act[:, (SSM_GROUPS + g) * SSM_STATE:(SSM_GROUPS + g + 1) * SSM_STATE].astype(BF16)
        cb = _dot_nt(cg, bg)
        s_old = s_ref[0, g * gw:(g + 1) * gw, :]
        y_inter = _dot_nt(cg, s_old.astype(BF16))
        pairs = []
        for j in range(hpg // 2):
            x_pair = xs_bf[:, g * gw + j * LANES:g * gw + (j + 1) * LANES]
            res = []
            for hh_ in range(2):
                h = g * hpg + 2 * j + hh_
                dec = jnp.exp(jnp.where(causal, cs[:, h:h + 1] - cs_t[h:h + 1, :], -jnp.inf))
                mm = (cb * dec * dt_t[h:h + 1, :]).astype(BF16)
                res.append(_dot(mm, x_pair))
            pairs.append(jnp.where(half, res[0], res[1]))
        y_intra = jnp.concatenate(pairs, axis=1)
        y_parts.append(y_intra + e_b[:, g * gw:(g + 1) * gw] * y_inter)
        upd = _dot_tn(xw_bf[:, g * gw:(g + 1) * gw], bg)
        for hh_ in range(hpg):
            h = g * hpg + hh_
            lo_r = g * gw + hh_ * SSM_HEADDIM
            s_ref[0, lo_r:lo_r + SSM_HEADDIM, :] = (
                ecs[CHUNK - 1:CHUNK, h:h + 1] * s_ref[0, lo_r:lo_r + SSM_HEADDIM, :]
                + upd[hh_ * SSM_HEADDIM:(hh_ + 1) * SSM_HEADDIM, :])
    y = jnp.concatenate(y_parts, axis=1)
    y_s = (y + dskip_ref[...] * xs_act) * _silu(z_ref[...])
    y_b = jnp.concatenate(
        [_rms(y_s[:, g * gw:(g + 1) * gw]) for g in range(SSM_GROUPS)], axis=1) * snorm_ref[...]

    gates = gates_ref[...]
    merged_ref[...] = (_sigmoid(gates[:, 0:D_MODEL]) * y_a
                       + _sigmoid(gates[:, D_MODEL:2 * D_MODEL]) * y_b).astype(merged_ref.dtype)


def _const_spec(shape, single_buffer=False):
    kwargs = dict(pipeline_mode=pl.Buffered(1)) if single_buffer else {}
    return pl.BlockSpec(shape, lambda *_: (0,) * len(shape), **kwargs)


def _mixer_params(w):
    return [w["bias_row"], w["alog_row"], w["ssm_conv_w"], w["ssm_conv_b"].reshape(1, XBC_DIM),
            w["dskip_row"], w["ml_head_norm"].reshape(1, D_MODEL), w["ssm_norm"].reshape(1, SSM_WIDTH),
            w["e1"]]


_MIXER_PARAM_SHAPES = [(1, LANES), (1, LANES), (CONV_W, XBC_DIM), (1, XBC_DIM), (1, SSM_WIDTH),
                       (1, D_MODEL), (1, SSM_WIDTH), (LANES, SSM_WIDTH)]


def _mixer_chunk(proj, w, *, n_seq, n_chunks):
    def seg(width, start):
        assert start % width == 0
        return pl.BlockSpec((CHUNK, width), lambda b, c: (b * n_chunks + c, start // width))

    def per_seq(shape):
        return pl.BlockSpec((1,) + shape, lambda b, c: (b,) + (0,) * len(shape))

    state_shapes = [(HIST, XBC_DIM), (ML_HEADS, ML_DK, ML_DV), (ML_HEADS, ML_DK), (1, LANES),
                    (SSM_WIDTH, SSM_STATE)]
    in_specs = [
        seg(2 * D_MODEL, SEG_GATES), seg(D_MODEL, SEG_OG), seg(SSM_WIDTH, SEG_Z), seg(D_MODEL, SEG_V),
        seg(SSM_WIDTH, SEG_XS), seg(ML_QK, SEG_Q), seg(ML_QK, SEG_K), seg(BC_DIM, SEG_BC),
        seg(LANES, SEG_SMALL),
    ] + [_const_spec(s) for s in _MIXER_PARAM_SHAPES]
    out_specs = [pl.BlockSpec((CHUNK, D_MODEL), lambda b, c: (b * n_chunks + c, 0))] + [
        per_seq(s) for s in state_shapes]
    out_shape = [jax.ShapeDtypeStruct((proj.shape[0], D_MODEL), BF16)] + [
        jax.ShapeDtypeStruct((n_seq,) + s, F32) for s in state_shapes]
    return pl.pallas_call(
        _mixer_kernel,
        grid=(n_seq, n_chunks),
        in_specs=in_specs,
        out_specs=out_specs,
        out_shape=out_shape,
        scratch_shapes=[pltpu.VMEM((SUBLANES, SSM_WIDTH), F32), pltpu.VMEM((SUBLANES, BC_DIM), F32)],
        compiler_params=_params(),
        name="mixer_chunk",
    )(*([proj] * 9), *_mixer_params(w))


def _step_kernel(proj_ref, conv0_ref, n0_ref, m0_ref, c0_ref, s0_ref,
                 bias_ref, alog_ref, convw_ref, convb_ref, dskip_ref, hnorm_ref, snorm_ref, e1_ref,
                 merged_ref, conv_ref, n_ref, m_ref, c_ref, s_ref,
                 qf_ref, kt_ref, vb_ref, cmf_ref, bb_ref, xwt_ref, dec_ref, accq_ref, accs_ref):
    b = pl.program_id(0)
    ns = proj_ref.shape[0]
    gw = SSM_WIDTH // SSM_GROUPS
    hpg = SSM_HEADS // SSM_GROUPS
    lane = lax.broadcasted_iota(jnp.int32, (ns, LANES), 1)
    lane1 = lax.broadcasted_iota(jnp.int32, (1, LANES), 1)
    is_dt = lane < SM_IG
    is_li = (lane >= SM_IG) & (lane < SM_FG)
    is_lf = (lane >= SM_FG) & (lane < SM_FG + ML_HEADS)

    def gate_scalars():
        pre = proj_ref[:, SEG_SMALL:SEG_SMALL + LANES] + bias_ref[...]
        dt = jnp.where(is_dt, _softplus(pre), 0.0)
        lf = pltpu.roll(jnp.where(is_lf, -_softplus(-pre), 0.0), LANES - ML_HEADS, axis=1)
        li = jnp.where(is_li, pre, 0.0)
        inter = lf + m0_ref[...]
        m_t = jnp.maximum(inter, li)
        a_row = jnp.where(lane1 < SM_IG, -jnp.exp(alog_ref[...]), 0.0)
        return dict(dt=dt, m_t=m_t, w_intra=jnp.exp(li - m_t), w_inter=jnp.exp(inter - m_t),
                    e=jnp.exp(dt * a_row))

    def conv_act(lo, hi, seg, store):
        x_new = proj_ref[:, seg:seg + hi - lo]
        st = [conv0_ref[:, j * XBC_DIM + lo:j * XBC_DIM + hi] for j in range(HIST)]
        acc = st[0] * convw_ref[0:1, lo:hi]
        for j in range(1, HIST):
            acc = acc + st[j] * convw_ref[j:j + 1, lo:hi]
        acc = acc + x_new * convw_ref[HIST:HIST + 1, lo:hi]
        if store:
            for j in range(1, HIST):
                conv_ref[:, (j - 1) * XBC_DIM + lo:(j - 1) * XBC_DIM + hi] = st[j]
            conv_ref[:, (HIST - 1) * XBC_DIM + lo:(HIST - 1) * XBC_DIM + hi] = x_new
        return _silu(acc + convb_ref[:, lo:hi])

    def head_col(x, h):
        return x[:, SM_IG + h:SM_IG + h + 1]

    @pl.when(b == 0)
    def _():
        gs = gate_scalars()
        m_ref[...] = gs["m_t"]
        dec_ref[...] = jnp.where(is_dt, gs["e"], gs["w_inter"])
        qf_ref[...] = (proj_ref[:, SEG_Q:SEG_Q + ML_QK] * (ML_DK ** -0.5)).astype(BF16).astype(F32)
        vb_ref[...] = proj_ref[:, SEG_V:SEG_V + D_MODEL].astype(BF16)
        for h in range(ML_HEADS):
            sl = slice(h * ML_DK, (h + 1) * ML_DK)
            kw = proj_ref[:, SEG_K + h * ML_DK:SEG_K + (h + 1) * ML_DK] * head_col(gs["w_intra"], h)
            n_ref[:, sl] = head_col(gs["w_inter"], h) * n0_ref[:, sl] + kw
            kt_ref[sl, :] = kw.T
        xs_act = conv_act(0, SSM_WIDTH, SEG_XS, True)
        bc_act = conv_act(SSM_WIDTH, XBC_DIM, SEG_BC, True)
        bb_ref[...] = bc_act[:, 0:SSM_GROUPS * SSM_STATE].astype(BF16)
        cmf_ref[...] = bc_act[:, SSM_GROUPS * SSM_STATE:BC_DIM]
        xw = xs_act * _expand_heads(gs["dt"], e1_ref[...])
        for g in range(SSM_GROUPS):
            xwt_ref[g * gw:(g + 1) * gw, :] = xw[:, g * gw:(g + 1) * gw].T
        accq_ref[...] = jnp.zeros(accq_ref.shape, F32)
        accs_ref[...] = jnp.zeros(accs_ref.shape, F32)

    for sb in range(c0_ref.shape[0]):
        seq = b * c0_ref.shape[0] + sb
        base = pl.multiple_of((seq // SUBLANES) * SUBLANES, SUBLANES)
        is_row = lax.broadcasted_iota(jnp.int32, (SUBLANES, 1), 0) == seq % SUBLANES
        is_seq = lax.broadcasted_iota(jnp.int32, (1, ns), 1) == seq
        q8 = jnp.where(is_row, qf_ref[pl.ds(base, SUBLANES), :], 0.0).astype(BF16)
        c8 = jnp.where(is_row, cmf_ref[pl.ds(base, SUBLANES), :], 0.0).astype(BF16)
        drow = dec_ref[pl.ds(seq, 1), :]
        for h in range(ML_HEADS):
            c_old = c0_ref[sb, h]
            accq_ref[pl.ds(base, SUBLANES), h * ML_DV:(h + 1) * ML_DV] += _dot(
                q8[:, h * ML_DK:(h + 1) * ML_DK], c_old.astype(BF16))
            kt = jnp.where(is_seq, kt_ref[h * ML_DK:(h + 1) * ML_DK, :], 0.0).astype(BF16)
            c_ref[sb, h] = (drow[:, SM_IG + h:SM_IG + h + 1] * c_old
                            + _dot(kt, vb_ref[:, h * ML_DV:(h + 1) * ML_DV]))
        for g in range(SSM_GROUPS):
            s_old = s0_ref[sb, g * gw:(g + 1) * gw, :]
            accs_ref[pl.ds(base, SUBLANES), g * gw:(g + 1) * gw] += _dot_nt(
                c8[:, g * SSM_STATE:(g + 1) * SSM_STATE], s_old.astype(BF16))
            xwt = jnp.where(is_seq, xwt_ref[g * gw:(g + 1) * gw, :], 0.0).astype(BF16)
            upd = _dot(xwt, bb_ref[:, g * SSM_STATE:(g + 1) * SSM_STATE])
            for hh_ in range(hpg):
                h = g * hpg + hh_
                rs = slice(hh_ * SSM_HEADDIM, (hh_ + 1) * SSM_HEADDIM)
                s_ref[sb, g * gw + hh_ * SSM_HEADDIM:g * gw + (hh_ + 1) * SSM_HEADDIM, :] = (
                    drow[:, h:h + 1] * s_old[rs, :] + upd[rs, :])

    @pl.when(b == pl.num_programs(0) - 1)
    def _():
        gs = gate_scalars()
        q = qf_ref[...]
        y_a_parts = []
        for h in range(ML_HEADS):
            qh = q[:, h * ML_DK:(h + 1) * ML_DK]
            kh = proj_ref[:, SEG_K + h * ML_DK:SEG_K + (h + 1) * ML_DK].astype(BF16).astype(F32)
            vh = vb_ref[:, h * ML_DV:(h + 1) * ML_DV].astype(F32)
            w_inter = head_col(gs["w_inter"], h)
            s = jnp.sum(qh * kh, axis=1, keepdims=True) * head_col(gs["w_intra"], h)
            qn = jnp.sum(qh * n0_ref[:, h * ML_DK:(h + 1) * ML_DK], axis=1, keepdims=True)
            num = s.astype(BF16).astype(F32) * vh + w_inter * accq_ref[:, h * ML_DV:(h + 1) * ML_DV]
            den = s + w_inter * qn
            hh = num / jnp.maximum(jnp.abs(den), jnp.exp(-head_col(gs["m_t"], h)))
            hn = _rms(hh) * hnorm_ref[:, h * ML_DV:(h + 1) * ML_DV]
            y_a_parts.append(_sigmoid(proj_ref[:, SEG_OG + h * ML_DV:SEG_OG + (h + 1) * ML_DV]) * hn)
        y_a = jnp.concatenate(y_a_parts, axis=1)

        xs_act = conv_act(0, SSM_WIDTH, SEG_XS, False)
        bb = bb_ref[...].astype(F32)
        cm = cmf_ref[...].astype(BF16).astype(F32)
        e1 = e1_ref[...]
        x_dt = xs_act * _expand_heads(gs["dt"], e1)
        e_e = _expand_heads(jnp.where(is_dt, gs["e"], 0.0), e1)
        y_parts = []
        for g in range(SSM_GROUPS):
            gl = slice(g * SSM_STATE, (g + 1) * SSM_STATE)
            cb = jnp.sum(cm[:, gl] * bb[:, gl], axis=1, keepdims=True)
            cl = slice(g * gw, (g + 1) * gw)
            y_parts.append(cb * x_dt[:, cl] + e_e[:, cl] * accs_ref[:, cl])
        y = jnp.concatenate(y_parts, axis=1)
        y_s = (y + dskip_ref[...] * xs_act) * _silu(proj_ref[:, SEG_Z:SEG_Z + SSM_WIDTH])
        y_b = jnp.concatenate(
            [_rms(y_s[:, g * gw:(g + 1) * gw]) for g in range(SSM_GROUPS)], axis=1) * snorm_ref[...]
        merged_ref[...] = (_sigmoid(proj_ref[:, SEG_GATES:SEG_GATES + D_MODEL]) * y_a
                           + _sigmoid(proj_ref[:, SEG_GATES + D_MODEL:SEG_GATES + 2 * D_MODEL]) * y_b)


def _mixer_step(proj, conv0, n0, m0, c0, s0, w, *, seq_per_step):
    ns = proj.shape[0]

    def per_seq(shape):
        return pl.BlockSpec((seq_per_step,) + shape, lambda b: (b,) + (0,) * len(shape))

    row_shapes = [(ns, HIST * XBC_DIM), (ns, ML_QK), (ns, LANES)]
    mat_shapes = [(ML_HEADS, ML_DK, ML_DV), (SSM_WIDTH, SSM_STATE)]
    in_specs = ([_const_spec((ns, PROJ_DIM), True)] + [_const_spec(s, True) for s in row_shapes]
                + [per_seq(s) for s in mat_shapes] + [_const_spec(s) for s in _MIXER_PARAM_SHAPES])
    out_specs = ([_const_spec((ns, D_MODEL))] + [_const_spec(s) for s in row_shapes]
                 + [per_seq(s) for s in mat_shapes])
    out_shape = ([jax.ShapeDtypeStruct((ns, D_MODEL), F32)]
                 + [jax.ShapeDtypeStruct(s, F32) for s in row_shapes]
                 + [jax.ShapeDtypeStruct((ns,) + s, F32) for s in mat_shapes])
    scratch = [
        pltpu.VMEM((ns, ML_QK), F32),
        pltpu.VMEM((ML_QK, ns), F32),
        pltpu.VMEM((ns, D_MODEL), BF16),
        pltpu.VMEM((ns, SSM_GROUPS * SSM_STATE), F32),
        pltpu.VMEM((ns, SSM_GROUPS * SSM_STATE), BF16),
        pltpu.VMEM((SSM_WIDTH, ns), F32),
        pltpu.VMEM((ns, LANES), F32),
        pltpu.VMEM((ns, D_MODEL), F32),
        pltpu.VMEM((ns, SSM_WIDTH), F32),
    ]
    return pl.pallas_call(
        _step_kernel,
        grid=(ns // seq_per_step,),
        in_specs=in_specs,
        out_specs=out_specs,
        out_shape=out_shape,
        scratch_shapes=scratch,
        compiler_params=_params(1),
        name="mixer_step",
    )(proj, conv0, n0, m0, c0, s0, *_mixer_params(w))


def _pad_lanes(vec, offset):
    pad = [(0, 0)] * (vec.ndim - 1) + [(offset, LANES - offset - vec.shape[-1])]
    return jnp.pad(vec.astype(F32), pad)


def kernel(x_prompt, x_sample, state_conv, state_mlstm_C, state_mlstm_n, state_mlstm_m, state_ssm, ffn1_norm, ffn1_w_gate, ffn1_w_up, ffn1_w_down, mix_norm, w_in, ml_i_bias, ml_f_bias, ml_head_norm, ssm_conv_w, ssm_conv_b, ssm_dt_bias, ssm_A_log, ssm_D, ssm_norm, w_out, ffn2_norm, ffn2_w_gate, ffn2_w_up, ffn2_w_down, final_norm):
    depth = w_in.shape[0]
    assert depth == 1, "the final norm is fused into the (single) layer's second FFN"
    bp, seq, d = x_prompt.shape
    bs = x_sample.shape[0]
    assert x_sample.shape[1] == 1 and seq % CHUNK == 0 and bs % SUBLANES == 0
    l = 0
    tm = 1024

    head_of_lane = jnp.arange(SSM_WIDTH) // SSM_HEADDIM
    w = {
        "bias_row": (_pad_lanes(ssm_dt_bias[l], SM_DT) + _pad_lanes(ml_i_bias[l], SM_IG)
                     + _pad_lanes(ml_f_bias[l], SM_FG)).reshape(1, LANES),
        "alog_row": _pad_lanes(ssm_A_log[l], SM_DT).reshape(1, LANES),
        "ml_head_norm": ml_head_norm[l], "ssm_conv_w": ssm_conv_w[l], "ssm_conv_b": ssm_conv_b[l],
        "dskip_row": jnp.repeat(ssm_D[l].astype(F32), SSM_HEADDIM).reshape(1, SSM_WIDTH),
        "ssm_norm": ssm_norm[l],
        "e1": (jnp.arange(LANES)[:, None] == head_of_lane[None, :]).astype(BF16),
    }

    xp = x_prompt.reshape(bp * seq, d)
    xs = x_sample.reshape(bs, d)
    xp1, xs1 = _ffn(xp, xs, ffn1_norm[l], ffn1_w_gate[l], ffn1_w_up[l], ffn1_w_down[l], final_norm,
                    final_norm=False, tm=tm, tf=256)
    w_in_r = _regroup_w_in(jnp.swapaxes(w_in[l], 0, 1), tc=256)
    proj_p, proj_s = _inproj(xp1, xs1, mix_norm[l], w_in_r, tm=tm, tn=PROJ_TILE)

    mp, p_conv, p_c, p_n, p_m, p_s = _mixer_chunk(proj_p, w, n_seq=bp, n_chunks=seq // CHUNK)
    ms, s_conv, s_n, s_m, s_c, s_s = _mixer_step(
        proj_s, state_conv[l].reshape(bs, HIST * XBC_DIM), state_mlstm_n[l].reshape(bs, ML_QK),
        _pad_lanes(state_mlstm_m[l], SM_IG), state_mlstm_C[l],
        state_ssm[l].reshape(bs, SSM_WIDTH, SSM_STATE), w, seq_per_step=2)

    xp2, xs2 = _outproj(mp, ms, w_out[l], xp1, xs1, tm=2 * tm, tn=512)
    yp, ys = _ffn(xp2, xs2, ffn2_norm[l], ffn2_w_gate[l], ffn2_w_up[l], ffn2_w_down[l], final_norm,
                  final_norm=True, tm=tm, tf=256)

    ssm_shape = (SSM_HEADS, SSM_HEADDIM, SSM_STATE)
    return (yp.reshape(bp, seq, d), ys.reshape(bs, 1, d),
            p_conv[None], p_c[None], p_n[None], p_m[:, 0, :ML_HEADS][None], p_s.reshape(bp, *ssm_shape)[None],
            s_conv.reshape(bs, HIST, XBC_DIM)[None], s_c[None], s_n.reshape(bs, ML_HEADS, ML_DK)[None],
            s_m[:, SM_IG:SM_IG + ML_HEADS][None], s_s.reshape(bs, *ssm_shape)[None])
```

```python
import functools

import numpy as np
import jax
import jax.numpy as jnp
from jax import lax
from jax.experimental import pallas as pl
from jax.experimental.pallas import tpu as pltpu

F32 = jnp.float32
BF16 = jnp.bfloat16

EPS = 1e-6
NEG_INIT = -1e30

D_MODEL = 2048
ML_HEADS = 4
ML_DV = 512
ML_DK = 256
ML_QK = ML_HEADS * ML_DK
SSM_HEADS = 32
SSM_HEADDIM = 64
SSM_GROUPS = 2
SSM_STATE = 128
SSM_WIDTH = SSM_HEADS * SSM_HEADDIM
CONV_W = 4
HIST = CONV_W - 1
BC_DIM = 2 * SSM_GROUPS * SSM_STATE
XBC_DIM = SSM_WIDTH + BC_DIM
CHUNK = 128
IN_SIZES = (ML_QK, ML_QK, D_MODEL, ML_HEADS, ML_HEADS, D_MODEL, SSM_WIDTH, XBC_DIM, SSM_HEADS, 2 * D_MODEL)

LANES = 128
SUBLANES = 8
MXU_WIDTH = 256

SEG_XS = 0
SEG_BC = SEG_XS + SSM_WIDTH
SEG_GATES = SEG_BC + BC_DIM
SEG_OG = SEG_GATES + 2 * D_MODEL
SEG_Z = SEG_OG + D_MODEL
SEG_V = SEG_Z + SSM_WIDTH
SEG_Q = SEG_V + D_MODEL
SEG_K = SEG_Q + ML_QK
SEG_SMALL = SEG_K + ML_QK
PROJ_TILE = 1280
PROJ_DIM = -(-(SEG_SMALL + LANES) // PROJ_TILE) * PROJ_TILE
assert SEG_XS == 0 and XBC_DIM % PROJ_TILE == 0
CONV_TILES = XBC_DIM // PROJ_TILE
ACT_TILES = -(-SEG_V // PROJ_TILE)
SM_DT = 0
SM_IG = SM_DT + SSM_HEADS
SM_FG = SM_IG + ML_HEADS

VMEM_LIMIT = 60 * 1024 * 1024


def _rms(x):
    return x * lax.rsqrt(jnp.mean(x * x, axis=-1, keepdims=True) + EPS)


def _sigmoid(x):
    return 0.5 * jnp.tanh(0.5 * x) + 0.5


def _silu(x):
    h = 0.5 * x
    return h * jnp.tanh(h) + h


def _softplus(x):
    return jnp.maximum(x, 0.0) + jnp.log(1.0 + jnp.exp(-jnp.abs(x)))


def _dot(a, b):
    return jnp.dot(a, b, preferred_element_type=F32)


def _dot_nt(a, b):
    return lax.dot_general(a, b, (((1,), (1,)), ((), ())), preferred_element_type=F32)


def _dot_tn(a, b):
    return lax.dot_general(a, b, (((0,), (0,)), ((), ())), preferred_element_type=F32)


def _split3(x):
    hi = x.astype(BF16)
    r1 = x - hi.astype(F32)
    mid = r1.astype(BF16)
    lo = (r1 - mid.astype(F32)).astype(BF16)
    return hi, mid, lo


def _expand_heads(x, e1):
    x_hi, x_mid, x_lo = _split3(x)
    return _dot(x_hi, e1) + _dot(x_mid, e1) + _dot(x_lo, e1)


def _params(dims=2):
    return pltpu.CompilerParams(dimension_semantics=("arbitrary",) * dims, vmem_limit_bytes=VMEM_LIMIT)


def _ffn_kernel(xp_ref, xs_ref, nw_ref, wg_ref, wu_ref, wd_ref, fw_ref, op_ref, os_ref, hp_ref, hs_ref,
                *, final_norm):
    i = pl.program_id(0)
    j = pl.program_id(1)
    last_j = pl.num_programs(1) - 1

    def rows(x_ref, o_ref, h_ref):
        @pl.when(j == 0)
        def _():
            h_ref[...] = (_rms(x_ref[...]) * nw_ref[...]).astype(BF16)
            o_ref[...] = jnp.zeros(o_ref.shape, F32)

        h = h_ref[...]
        g = _dot(h, wg_ref[...].astype(BF16))
        u = _dot(h, wu_ref[...].astype(BF16))
        a = (_silu(g) * u).astype(BF16)
        o_ref[...] += _dot(a, wd_ref[...].astype(BF16))

        @pl.when(j == last_j)
        def _():
            strip = min(x_ref.shape[0], LANES)

            def finish(r, carry):
                rows_ = pl.ds(pl.multiple_of(r * strip, strip), strip)
                y = x_ref[rows_, :] + 0.5 * o_ref[rows_, :]
                if final_norm:
                    y = _rms(y) * fw_ref[...]
                o_ref[rows_, :] = y
                return carry

            lax.fori_loop(0, x_ref.shape[0] // strip, finish, 0)

    rows(xp_ref, op_ref, hp_ref)

    @pl.when(i == pl.num_programs(0) - 1)
    def _():
        rows(xs_ref, os_ref, hs_ref)


def _ffn(xp, xs, norm_w, w_gate, w_up, w_down, final_w, *, final_norm, tm, tf):
    mp, d = xp.shape
    ms = xs.shape[0]
    f = w_gate.shape[1]
    return pl.pallas_call(
        functools.partial(_ffn_kernel, final_norm=final_norm),
        grid=(mp // tm, f // tf),
        in_specs=[
            pl.BlockSpec((tm, d), lambda i, j: (i, 0)),
            pl.BlockSpec((ms, d), lambda i, j: (0, 0), pipeline_mode=pl.Buffered(1)),
            pl.BlockSpec((1, d), lambda i, j: (0, 0)),
            pl.BlockSpec((d, tf), lambda i, j: (0, j)),
            pl.BlockSpec((d, tf), lambda i, j: (0, j)),
            pl.BlockSpec((tf, d), lambda i, j: (j, 0)),
            pl.BlockSpec((1, d), lambda i, j: (0, 0)),
        ],
        out_specs=[pl.BlockSpec((tm, d), lambda i, j: (i, 0)),
                   pl.BlockSpec((ms, d), lambda i, j: (0, 0))],
        out_shape=[jax.ShapeDtypeStruct((mp, d), F32), jax.ShapeDtypeStruct((ms, d), F32)],
        scratch_shapes=[pltpu.VMEM((tm, d), BF16), pltpu.VMEM((ms, d), BF16)],
        compiler_params=_params(),
        name="ffn",
    )(xp, xs, norm_w.reshape(1, d), w_gate, w_up, w_down, final_w.reshape(1, d))


def _causal_conv(cur, prev, w_ref, b_ref):
    row8 = lax.broadcasted_iota(jnp.int32, (SUBLANES, 1), 0)
    acc = cur * w_ref[HIST:HIST + 1, :] + b_ref[...]
    for back in range(1, CONV_W):
        sh = pltpu.roll(cur, back, axis=0)
        head = jnp.where(row8 < back, pltpu.roll(prev, back, axis=0), sh[0:SUBLANES])
        sh = jnp.concatenate([head, sh[SUBLANES:]], axis=0)
        acc = acc + sh * w_ref[HIST - back:HIST - back + 1, :]
    return acc


def _inproj_kernel(xp_ref, xs_ref, nw_ref, w_ref, act_ref, cw_ref, cb_ref, op_ref, os_ref, tail_ref,
                   up_ref, us_ref, hist_ref, raw_ref, *, blocks_per_seq):
    i = pl.program_id(0)
    j = pl.program_id(1)
    tm = xp_ref.shape[0]

    @pl.when(j == 0)
    def _():
        up_ref[...] = (_rms(xp_ref[...]) * nw_ref[...]).astype(BF16)

    @pl.when((i == 0) & (j == 0))
    def _():
        hist_ref[...] = jnp.zeros(hist_ref.shape, F32)

    def activate(x, cols):
        return ((act_ref[0:1, cols] + act_ref[1:2, cols] * x) * _sigmoid(x)
                + act_ref[2:3, cols] * x)

    is_conv = j < CONV_TILES
    is_act = (j >= CONV_TILES) & (j < ACT_TILES)
    strips = [slice(s, s + MXU_WIDTH) for s in range(0, w_ref.shape[0], MXU_WIDTH)]

    @pl.when(is_conv)
    def _():
        jj = jnp.minimum(j, CONV_TILES - 1)
        fresh = i % blocks_per_seq == 0
        for n, cols in enumerate(strips):
            raw_ref[n % 2] = _dot_nt(up_ref[...], w_ref[cols, :])
            raw = raw_ref[n % 2]
            prev = jnp.where(fresh, 0.0, hist_ref[jj, :, cols])
            tail = raw[tm - SUBLANES:tm]
            hist_ref[jj, :, cols] = tail
            tail_ref[0, :, cols] = tail
            op_ref[:, cols] = _silu(_causal_conv(raw, prev, cw_ref.at[:, cols], cb_ref.at[:, cols]))

    @pl.when(is_act)
    def _():
        for n, cols in enumerate(strips):
            raw_ref[n % 2] = _dot_nt(up_ref[...], w_ref[cols, :])
            op_ref[:, cols] = activate(raw_ref[n % 2], cols)

    @pl.when(j >= ACT_TILES)
    def _():
        op_ref[...] = _dot_nt(up_ref[...], w_ref[...])

    @pl.when(i == pl.num_programs(0) - 1)
    def _():
        @pl.when(j == 0)
        def _():
            us_ref[...] = (_rms(xs_ref[...]) * nw_ref[...]).astype(BF16)

        raw = _dot_nt(us_ref[...], w_ref[...])
        os_ref[...] = jnp.where(is_act, activate(raw, slice(None)), raw)


def _inproj(xp, xs, norm_w, wt, act, conv_w, conv_b, *, tm, tn, rows_per_seq):
    mp, d = xp.shape
    ms = xs.shape[0]
    n = wt.shape[0]
    last_i = mp // tm - 1
    blocks_per_seq = rows_per_seq // tm

    def conv_tile(i, j):
        return (0, jnp.minimum(j, CONV_TILES - 1))

    return pl.pallas_call(
        functools.partial(_inproj_kernel, blocks_per_seq=blocks_per_seq),
        grid=(mp // tm, n // tn),
        in_specs=[
            pl.BlockSpec((tm, d), lambda i, j: (i, 0)),
            pl.BlockSpec((ms, d), lambda i, j: (0, 0), pipeline_mode=pl.Buffered(1)),
            pl.BlockSpec((1, d), lambda i, j: (0, 0)),
            pl.BlockSpec((tn, d), lambda i, j: (j, 0)),
            pl.BlockSpec((SUBLANES, tn), lambda i, j: (0, j)),
            pl.BlockSpec((CONV_W, tn), conv_tile),
            pl.BlockSpec((1, tn), conv_tile),
        ],
        out_specs=[pl.BlockSpec((tm, tn), lambda i, j: (i, j)),
                   pl.BlockSpec((ms, tn), lambda i, j: (0, jnp.where(i == last_i, j, 0))),
                   pl.BlockSpec((1, SUBLANES, tn), lambda i, j: (i, 0, jnp.minimum(j, CONV_TILES - 1)))],
        out_shape=[jax.ShapeDtypeStruct((mp, n), F32), jax.ShapeDtypeStruct((ms, n), F32),
                   jax.ShapeDtypeStruct((mp // tm, SUBLANES, XBC_DIM), F32)],
        scratch_shapes=[pltpu.VMEM((tm, d), BF16), pltpu.VMEM((ms, d), BF16),
                        pltpu.VMEM((CONV_TILES, SUBLANES, tn), F32),
                        pltpu.VMEM((2, tm, MXU_WIDTH), F32)],
        compiler_params=_params(),
        name="inproj",
    )(xp, xs, norm_w.reshape(1, d), wt, act, conv_w, conv_b)


def _outproj_kernel(yp_ref, ys_ref, w_ref, xp_ref, xs_ref, op_ref, os_ref):
    w = w_ref[...].astype(BF16)
    op_ref[...] = xp_ref[...] + _dot(yp_ref[...].astype(BF16), w)

    @pl.when(pl.program_id(0) == pl.num_programs(0) - 1)
    def _():
        os_ref[...] = xs_ref[...] + _dot(ys_ref[...].astype(BF16), w)


def _outproj(yp, ys, w, xp, xs, *, tm, tn):
    mp, d = yp.shape
    ms = ys.shape[0]
    n = w.shape[1]
    last_i = mp // tm - 1

    def sample_tile(i, j):
        return (0, jnp.where(i == last_i, j, 0))

    return pl.pallas_call(
        _outproj_kernel,
        grid=(mp // tm, n // tn),
        in_specs=[
            pl.BlockSpec((tm, d), lambda i, j: (i, 0)),
            pl.BlockSpec((ms, d), lambda i, j: (0, 0)),
            pl.BlockSpec((d, tn), lambda i, j: (0, j)),
            pl.BlockSpec((tm, tn), lambda i, j: (i, j)),
            pl.BlockSpec((ms, tn), sample_tile),
        ],
        out_specs=[pl.BlockSpec((tm, tn), lambda i, j: (i, j)),
                   pl.BlockSpec((ms, tn), sample_tile)],
        out_shape=[jax.ShapeDtypeStruct((mp, n), F32), jax.ShapeDtypeStruct((ms, n), F32)],
        compiler_params=_params(),
        name="outproj",
    )(yp, ys, w, xp, xs)


def _seg_sources():
    offs = [0]
    for s in IN_SIZES:
        offs.append(offs[-1] + s)
    q, k, v, ig, fg, og, z, xbc, dtr, gates = offs[:-1]
    assert fg == ig + ML_HEADS and SM_FG == SM_IG + ML_HEADS
    return [(SEG_GATES, gates, 2 * D_MODEL), (SEG_OG, og, D_MODEL), (SEG_Z, z, SSM_WIDTH),
            (SEG_V, v, D_MODEL), (SEG_XS, xbc, SSM_WIDTH), (SEG_Q, q, ML_QK), (SEG_K, k, ML_QK),
            (SEG_BC, xbc + SSM_WIDTH, BC_DIM), (SEG_SMALL + SM_DT, dtr, SSM_HEADS),
            (SEG_SMALL + SM_IG, ig, 2 * ML_HEADS)]


def _activation_rows():
    col = np.arange(PROJ_DIM)
    sig = (col >= SEG_GATES) & (col < SEG_Z)
    silu = (col >= SEG_Z) & (col < SEG_V)
    rows = np.zeros((SUBLANES, PROJ_DIM), np.float32)
    rows[0], rows[1], rows[2] = sig, silu, ~(sig | silu)
    return jnp.asarray(rows)


def _regroup_kernel(wt_ref, o_ref):
    cols = wt_ref.shape[1]
    small = []
    for dst, src, width in _seg_sources():
        assert src % SUBLANES == 0 and width % SUBLANES == 0
        if dst < SEG_SMALL:
            o_ref[dst:dst + width, :] = wt_ref[src:src + width, :].astype(BF16)
        else:
            assert dst - SEG_SMALL == sum(p.shape[0] for p in small)
            small.append(wt_ref[src:src + width, :])
    used = sum(p.shape[0] for p in small)
    small.append(jnp.zeros((PROJ_DIM - SEG_SMALL - used, cols), F32))
    o_ref[SEG_SMALL:PROJ_DIM, :] = jnp.concatenate(small, axis=0).astype(BF16)


def _regroup_w_in(wt, *, tc):
    n_in, d = wt.shape
    return pl.pallas_call(
        _regroup_kernel,
        grid=(d // tc,),
        in_specs=[pl.BlockSpec((n_in, tc), lambda i: (0, i))],
        out_specs=pl.BlockSpec((PROJ_DIM, tc), lambda i: (0, i)),
        out_shape=jax.ShapeDtypeStruct((PROJ_DIM, d), BF16),
        compiler_params=_params(1),
        name="regroup",
    )(wt)


def _mixer_kernel(proj_ref, bias_ref, alog_ref, dskip_ref, hnorm_ref, snorm_ref, e1_ref,
                  merged_ref, c_ref, n_ref, m_ref, s_ref):
    c = pl.program_id(1)

    @pl.when(c == 0)
    def _():
        c_ref[...] = jnp.zeros(c_ref.shape, F32)
        n_ref[...] = jnp.zeros(n_ref.shape, F32)
        m_ref[...] = jnp.full(m_ref.shape, NEG_INIT, F32)
        s_ref[...] = jnp.zeros(s_ref.shape, F32)

    xs_act = proj_ref[:, SEG_XS:SEG_XS + SSM_WIDTH]
    bc_act = proj_ref[:, SEG_BC:SEG_BC + BC_DIM]

    lane = lax.broadcasted_iota(jnp.int32, (CHUNK, LANES), 1)
    row = lax.broadcasted_iota(jnp.int32, (CHUNK, LANES), 0)
    lane1 = lax.broadcasted_iota(jnp.int32, (1, LANES), 1)
    is_dt = lane < SM_IG
    is_li = (lane >= SM_IG) & (lane < SM_FG)
    is_lf = (lane >= SM_FG) & (lane < SM_FG + ML_HEADS)
    pre = proj_ref[:, SEG_SMALL:SEG_SMALL + LANES] + bias_ref[...]
    dt = jnp.where(is_dt, _softplus(pre), 0.0)
    lf = jnp.where(is_lf, -_softplus(-pre), 0.0)
    li = jnp.where(is_li, pre, 0.0)
    a_row = jnp.where(lane1 < SM_IG, -jnp.exp(alog_ref[...]), 0.0)
    causal = row >= lane
    tril = jnp.where(causal, 1.0, 0.0).astype(BF16)
    hi, mid, lo = _split3(dt * a_row + lf)
    cs = _dot(tril, hi) + _dot(tril, mid) + _dot(tril, lo)
    cs_t = cs.T
    li_t = li.T
    dt_t = dt.T
    b_last = cs[CHUNK - 1:CHUNK, :]
    ecs = jnp.exp(cs)

    q_all = (proj_ref[:, SEG_Q:SEG_Q + ML_QK] * (ML_DK ** -0.5)).astype(BF16)
    k_all = proj_ref[:, SEG_K:SEG_K + ML_QK]
    v_all = proj_ref[:, SEG_V:SEG_V + D_MODEL].astype(BF16)
    og_gate = proj_ref[:, SEG_OG:SEG_OG + D_MODEL]
    m_old = m_ref[0]
    m_new_row = m_old
    y_a_parts = []
    for h in range(ML_HEADS):
        ck = SM_FG + h
        b_col = cs[:, ck:ck + 1]
        b_row = cs_t[ck:ck + 1, :]
        li_row = li_t[SM_IG + h:SM_IG + h + 1, :]
        li_col = li[:, SM_IG + h:SM_IG + h + 1]
        m_prev = m_old[:, h:h + 1]
        dmat = jnp.where(causal, b_col - b_row + li_row, -jnp.inf)
        inter = b_col + m_prev
        m_t = jnp.maximum(inter, jnp.max(dmat, axis=1, keepdims=True))
        w_intra = jnp.exp(dmat - m_t)
        w_inter = jnp.exp(inter - m_t)
        qh = q_all[:, h * ML_DK:(h + 1) * ML_DK]
        kh = k_all[:, h * ML_DK:(h + 1) * ML_DK]
        vh = v_all[:, h * ML_DV:(h + 1) * ML_DV]
        c_old = c_ref[0, h]
        n_old = n_ref[0, h:h + 1, :]
        s = _dot_nt(qh, kh.astype(BF16)) * w_intra
        num = _dot(s.astype(BF16), vh) + w_inter * _dot(qh, c_old.astype(BF16))
        qn = jnp.sum(qh.astype(F32) * n_old, axis=1, keepdims=True)
        den = jnp.sum(s, axis=1, keepdims=True) + w_inter * qn
        hh = num / jnp.maximum(jnp.abs(den), jnp.exp(-m_t))
        m_new = m_t[CHUNK - 1:CHUNK, :]
        bl = b_col[CHUNK - 1:CHUNK, :]
        w_end = jnp.exp(bl - b_col + li_col - m_new)
        decay = jnp.exp(bl + m_prev - m_new)
        kw = kh * w_end
        c_ref[0, h] = decay * c_old + _dot_tn(kw.astype(BF16), vh)
        n_ref[0, h:h + 1, :] = decay * n_old + jnp.sum(kw, axis=0, keepdims=True)
        m_new_row = jnp.where(lane1 == h, m_new, m_new_row)
        hn = _rms(hh) * hnorm_ref[:, h * ML_DV:(h + 1) * ML_DV]
        y_a_parts.append(og_gate[:, h * ML_DV:(h + 1) * ML_DV] * hn)
    m_ref[0] = m_new_row
    y_a = jnp.concatenate(y_a_parts, axis=1)

    e1 = e1_ref[...]
    e_b = _expand_heads(jnp.where(is_dt, ecs, 0.0), e1)
    e_w = _expand_heads(jnp.exp(b_last - cs) * dt, e1)
    xs_bf = xs_act.astype(BF16)
    xw_bf = (xs_act * e_w).astype(BF16)
    half = lane < SSM_HEADDIM
    gw = SSM_WIDTH // SSM_GROUPS
    hpg = SSM_HEADS // SSM_GROUPS
    y_parts = []
    for g in range(SSM_GROUPS):
        bg = bc_act[:, g * SSM_STATE:(g + 1) * SSM_STATE].astype(BF16)
        cg = bc_act[:, (SSM_GROUPS + g) * SSM_STATE:(SSM_GROUPS + g + 1) * SSM_STATE].astype(BF16)
        cb = _dot_nt(cg, bg)
        s_old = s_ref[0, g * gw:(g + 1) * gw, :]
        y_inter = _dot_nt(cg, s_old.astype(BF16))
        pairs = []
        for j in range(hpg // 2):
            x_pair = xs_bf[:, g * gw + j * LANES:g * gw + (j + 1) * LANES]
            res = []
            for hh_ in range(2):
                h = g * hpg + 2 * j + hh_
                dec = jnp.exp(jnp.where(causal, cs[:, h:h + 1] - cs_t[h:h + 1, :], -jnp.inf))
                mm = (cb * dec * dt_t[h:h + 1, :]).astype(BF16)
                res.append(_dot(mm, x_pair))
            pairs.append(jnp.where(half, res[0], res[1]))
        y_intra = jnp.concatenate(pairs, axis=1)
        y_parts.append(y_intra + e_b[:, g * gw:(g + 1) * gw] * y_inter)
        upd = _dot_tn(xw_bf[:, g * gw:(g + 1) * gw], bg)
        for hh_ in range(hpg):
            h = g * hpg + hh_
            lo_r = g * gw + hh_ * SSM_HEADDIM
            s_ref[0, lo_r:lo_r + SSM_HEADDIM, :] = (
                ecs[CHUNK - 1:CHUNK, h:h + 1] * s_ref[0, lo_r:lo_r + SSM_HEADDIM, :]
                + upd[hh_ * SSM_HEADDIM:(hh_ + 1) * SSM_HEADDIM, :])
    y = jnp.concatenate(y_parts, axis=1)
    y_s = (y + dskip_ref[...] * xs_act) * proj_ref[:, SEG_Z:SEG_Z + SSM_WIDTH]
    y_b = jnp.concatenate(
        [_rms(y_s[:, g * gw:(g + 1) * gw]) for g in range(SSM_GROUPS)], axis=1) * snorm_ref[...]

    merged_ref[...] = (proj_ref[:, SEG_GATES:SEG_GATES + D_MODEL] * y_a
                       + proj_ref[:, SEG_GATES + D_MODEL:SEG_GATES + 2 * D_MODEL] * y_b
                       ).astype(merged_ref.dtype)


def _const_spec(shape, single_buffer=False):
    kwargs = dict(pipeline_mode=pl.Buffered(1)) if single_buffer else {}
    return pl.BlockSpec(shape, lambda *_: (0,) * len(shape), **kwargs)


def _mixer_params(w):
    return [w["bias_row"], w["alog_row"], w["dskip_row"], w["ml_head_norm"].reshape(1, D_MODEL),
            w["ssm_norm"].reshape(1, SSM_WIDTH), w["e1"]]


_MIXER_PARAM_SHAPES = [(1, LANES), (1, LANES), (1, SSM_WIDTH), (1, D_MODEL), (1, SSM_WIDTH),
                       (LANES, SSM_WIDTH)]


def _mixer_chunk(proj, w, *, n_seq, n_chunks):
    def per_seq(shape):
        return pl.BlockSpec((1,) + shape, lambda b, c: (b,) + (0,) * len(shape))

    state_shapes = [(ML_HEADS, ML_DK, ML_DV), (ML_HEADS, ML_DK), (1, LANES), (SSM_WIDTH, SSM_STATE)]
    in_specs = [pl.BlockSpec((CHUNK, PROJ_DIM), lambda b, c: (b * n_chunks + c, 0))] + [
        _const_spec(s) for s in _MIXER_PARAM_SHAPES]
    out_specs = [pl.BlockSpec((CHUNK, D_MODEL), lambda b, c: (b * n_chunks + c, 0))] + [
        per_seq(s) for s in state_shapes]
    out_shape = [jax.ShapeDtypeStruct((proj.shape[0], D_MODEL), BF16)] + [
        jax.ShapeDtypeStruct((n_seq,) + s, F32) for s in state_shapes]
    return pl.pallas_call(
        _mixer_kernel,
        grid=(n_seq, n_chunks),
        in_specs=in_specs,
        out_specs=out_specs,
        out_shape=out_shape,
        compiler_params=_params(),
        name="mixer_chunk",
    )(proj, *_mixer_params(w))


def _step_kernel(proj_ref, conv0_ref, n0_ref, m0_ref, c0_ref, s0_ref,
                 bias_ref, alog_ref, dskip_ref, hnorm_ref, snorm_ref, e1_ref, convw_ref, convb_ref,
                 merged_ref, conv_ref, n_ref, m_ref, c_ref, s_ref,
                 qf_ref, kt_ref, vb_ref, cmf_ref, bb_ref, xwt_ref, dec_ref, accq_ref, accs_ref):
    b = pl.program_id(0)
    ns = proj_ref.shape[0]
    gw = SSM_WIDTH // SSM_GROUPS
    hpg = SSM_HEADS // SSM_GROUPS
    lane = lax.broadcasted_iota(jnp.int32, (ns, LANES), 1)
    lane1 = lax.broadcasted_iota(jnp.int32, (1, LANES), 1)
    is_dt = lane < SM_IG
    is_li = (lane >= SM_IG) & (lane < SM_FG)
    is_lf = (lane >= SM_FG) & (lane < SM_FG + ML_HEADS)

    def gate_scalars():
        pre = proj_ref[:, SEG_SMALL:SEG_SMALL + LANES] + bias_ref[...]
        dt = jnp.where(is_dt, _softplus(pre), 0.0)
        lf = pltpu.roll(jnp.where(is_lf, -_softplus(-pre), 0.0), LANES - ML_HEADS, axis=1)
        li = jnp.where(is_li, pre, 0.0)
        inter = lf + m0_ref[...]
        m_t = jnp.maximum(inter, li)
        a_row = jnp.where(lane1 < SM_IG, -jnp.exp(alog_ref[...]), 0.0)
        return dict(dt=dt, m_t=m_t, w_intra=jnp.exp(li - m_t), w_inter=jnp.exp(inter - m_t),
                    e=jnp.exp(dt * a_row))

    def conv_act(lo, hi, seg, store):
        x_new = proj_ref[:, seg:seg + hi - lo]
        st = [conv0_ref[:, j * XBC_DIM + lo:j * XBC_DIM + hi] for j in range(HIST)]
        acc = st[0] * convw_ref[0:1, lo:hi]
        for j in range(1, HIST):
            acc = acc + st[j] * convw_ref[j:j + 1, lo:hi]
        acc = acc + x_new * convw_ref[HIST:HIST + 1, lo:hi]
        if store:
            for j in range(1, HIST):
                conv_ref[:, (j - 1) * XBC_DIM + lo:(j - 1) * XBC_DIM + hi] = st[j]
            conv_ref[:, (HIST - 1) * XBC_DIM + lo:(HIST - 1) * XBC_DIM + hi] = x_new
        return _silu(acc + convb_ref[:, lo:hi])

    def head_col(x, h):
        return x[:, SM_IG + h:SM_IG + h + 1]

    @pl.when(b == 0)
    def _():
        gs = gate_scalars()
        m_ref[...] = gs["m_t"]
        dec_ref[...] = jnp.where(is_dt, gs["e"], gs["w_inter"])
        qf_ref[...] = (proj_ref[:, SEG_Q:SEG_Q + ML_QK] * (ML_DK ** -0.5)).astype(BF16).astype(F32)
        vb_ref[...] = proj_ref[:, SEG_V:SEG_V + D_MODEL].astype(BF16)
        for h in range(ML_HEADS):
            sl = slice(h * ML_DK, (h + 1) * ML_DK)
            kw = proj_ref[:, SEG_K + h * ML_DK:SEG_K + (h + 1) * ML_DK] * head_col(gs["w_intra"], h)
            n_ref[:, sl] = head_col(gs["w_inter"], h) * n0_ref[:, sl] + kw
            kt_ref[sl, :] = kw.T
        xs_act = conv_act(0, SSM_WIDTH, SEG_XS, True)
        bc_act = conv_act(SSM_WIDTH, XBC_DIM, SEG_BC, True)
        bb_ref[...] = bc_act[:, 0:SSM_GROUPS * SSM_STATE].astype(BF16)
        cmf_ref[...] = bc_act[:, SSM_GROUPS * SSM_STATE:BC_DIM]
        xw = xs_act * _expand_heads(gs["dt"], e1_ref[...])
        for g in range(SSM_GROUPS):
            xwt_ref[g * gw:(g + 1) * gw, :] = xw[:, g * gw:(g + 1) * gw].T
        accq_ref[...] = jnp.zeros(accq_ref.shape, F32)
        accs_ref[...] = jnp.zeros(accs_ref.shape, F32)

    for sb in range(c0_ref.shape[0]):
        seq = b * c0_ref.shape[0] + sb
        base = pl.multiple_of((seq // SUBLANES) * SUBLANES, SUBLANES)
        is_row = lax.broadcasted_iota(jnp.int32, (SUBLANES, 1), 0) == seq % SUBLANES
        is_seq = lax.broadcasted_iota(jnp.int32, (1, ns), 1) == seq
        q8 = jnp.where(is_row, qf_ref[pl.ds(base, SUBLANES), :], 0.0).astype(BF16)
        c8 = jnp.where(is_row, cmf_ref[pl.ds(base, SUBLANES), :], 0.0).astype(BF16)
        drow = dec_ref[pl.ds(seq, 1), :]
        for h in range(ML_HEADS):
            c_old = c0_ref[sb, h]
            accq_ref[pl.ds(base, SUBLANES), h * ML_DV:(h + 1) * ML_DV] += _dot(
                q8[:, h * ML_DK:(h + 1) * ML_DK], c_old.astype(BF16))
            kt = jnp.where(is_seq, kt_ref[h * ML_DK:(h + 1) * ML_DK, :], 0.0).astype(BF16)
            c_ref[sb, h] = (drow[:, SM_IG + h:SM_IG + h + 1] * c_old
                            + _dot(kt, vb_ref[:, h * ML_DV:(h + 1) * ML_DV]))
        for g in range(SSM_GROUPS):
            s_old = s0_ref[sb, g * gw:(g + 1) * gw, :]
            accs_ref[pl.ds(base, SUBLANES), g * gw:(g + 1) * gw] += _dot_nt(
                c8[:, g * SSM_STATE:(g + 1) * SSM_STATE], s_old.astype(BF16))
            xwt = jnp.where(is_seq, xwt_ref[g * gw:(g + 1) * gw, :], 0.0).astype(BF16)
            upd = _dot(xwt, bb_ref[:, g * SSM_STATE:(g + 1) * SSM_STATE])
            for hh_ in range(hpg):
                h = g * hpg + hh_
                rs = slice(hh_ * SSM_HEADDIM, (hh_ + 1) * SSM_HEADDIM)
                s_ref[sb, g * gw + hh_ * SSM_HEADDIM:g * gw + (hh_ + 1) * SSM_HEADDIM, :] = (
                    drow[:, h:h + 1] * s_old[rs, :] + upd[rs, :])

    @pl.when(b == pl.num_programs(0) - 1)
    def _():
        gs = gate_scalars()
        q = qf_ref[...]
        y_a_parts = []
        for h in range(ML_HEADS):
            qh = q[:, h * ML_DK:(h + 1) * ML_DK]
            kh = proj_ref[:, SEG_K + h * ML_DK:SEG_K + (h + 1) * ML_DK].astype(BF16).astype(F32)
            vh = vb_ref[:, h * ML_DV:(h + 1) * ML_DV].astype(F32)
            w_inter = head_col(gs["w_inter"], h)
            s = jnp.sum(qh * kh, axis=1, keepdims=True) * head_col(gs["w_intra"], h)
            qn = jnp.sum(qh * n0_ref[:, h * ML_DK:(h + 1) * ML_DK], axis=1, keepdims=True)
            num = s.astype(BF16).astype(F32) * vh + w_inter * accq_ref[:, h * ML_DV:(h + 1) * ML_DV]
            den = s + w_inter * qn
            hh = num / jnp.maximum(jnp.abs(den), jnp.exp(-head_col(gs["m_t"], h)))
            hn = _rms(hh) * hnorm_ref[:, h * ML_DV:(h + 1) * ML_DV]
            y_a_parts.append(proj_ref[:, SEG_OG + h * ML_DV:SEG_OG + (h + 1) * ML_DV] * hn)
        y_a = jnp.concatenate(y_a_parts, axis=1)

        xs_act = conv_act(0, SSM_WIDTH, SEG_XS, False)
        bb = bb_ref[...].astype(F32)
        cm = cmf_ref[...].astype(BF16).astype(F32)
        e1 = e1_ref[...]
        x_dt = xs_act * _expand_heads(gs["dt"], e1)
        e_e = _expand_heads(jnp.where(is_dt, gs["e"], 0.0), e1)
        y_parts = []
        for g in range(SSM_GROUPS):
            gl = slice(g * SSM_STATE, (g + 1) * SSM_STATE)
            cb = jnp.sum(cm[:, gl] * bb[:, gl], axis=1, keepdims=True)
            cl = slice(g * gw, (g + 1) * gw)
            y_parts.append(cb * x_dt[:, cl] + e_e[:, cl] * accs_ref[:, cl])
        y = jnp.concatenate(y_parts, axis=1)
        y_s = (y + dskip_ref[...] * xs_act) * proj_ref[:, SEG_Z:SEG_Z + SSM_WIDTH]
        y_b = jnp.concatenate(
            [_rms(y_s[:, g * gw:(g + 1) * gw]) for g in range(SSM_GROUPS)], axis=1) * snorm_ref[...]
        merged_ref[...] = (proj_ref[:, SEG_GATES:SEG_GATES + D_MODEL] * y_a
                           + proj_ref[:, SEG_GATES + D_MODEL:SEG_GATES + 2 * D_MODEL] * y_b)


def _mixer_step(proj, conv0, n0, m0, c0, s0, w, *, seq_per_step):
    ns = proj.shape[0]

    def per_seq(shape):
        return pl.BlockSpec((seq_per_step,) + shape, lambda b: (b,) + (0,) * len(shape))

    row_shapes = [(ns, HIST * XBC_DIM), (ns, ML_QK), (ns, LANES)]
    mat_shapes = [(ML_HEADS, ML_DK, ML_DV), (SSM_WIDTH, SSM_STATE)]
    conv_shapes = [(CONV_W, XBC_DIM), (1, XBC_DIM)]
    in_specs = ([_const_spec((ns, PROJ_DIM), True)] + [_const_spec(s, True) for s in row_shapes]
                + [per_seq(s) for s in mat_shapes]
                + [_const_spec(s) for s in _MIXER_PARAM_SHAPES + conv_shapes])
    out_specs = ([_const_spec((ns, D_MODEL))] + [_const_spec(s) for s in row_shapes]
                 + [per_seq(s) for s in mat_shapes])
    out_shape = ([jax.ShapeDtypeStruct((ns, D_MODEL), F32)]
                 + [jax.ShapeDtypeStruct(s, F32) for s in row_shapes]
                 + [jax.ShapeDtypeStruct((ns,) + s, F32) for s in mat_shapes])
    scratch = [
        pltpu.VMEM((ns, ML_QK), F32),
        pltpu.VMEM((ML_QK, ns), F32),
        pltpu.VMEM((ns, D_MODEL), BF16),
        pltpu.VMEM((ns, SSM_GROUPS * SSM_STATE), F32),
        pltpu.VMEM((ns, SSM_GROUPS * SSM_STATE), BF16),
        pltpu.VMEM((SSM_WIDTH, ns), F32),
        pltpu.VMEM((ns, LANES), F32),
        pltpu.VMEM((ns, D_MODEL), F32),
        pltpu.VMEM((ns, SSM_WIDTH), F32),
    ]
    return pl.pallas_call(
        _step_kernel,
        grid=(ns // seq_per_step,),
        in_specs=in_specs,
        out_specs=out_specs,
        out_shape=out_shape,
        scratch_shapes=scratch,
        compiler_params=_params(1),
        name="mixer_step",
    )(proj, conv0, n0, m0, c0, s0, *_mixer_params(w), w["ssm_conv_w"], w["ssm_conv_b"])


def _pad_lanes(vec, offset):
    pad = [(0, 0)] * (vec.ndim - 1) + [(offset, LANES - offset - vec.shape[-1])]
    return jnp.pad(vec.astype(F32), pad)


def kernel(x_prompt, x_sample, state_conv, state_mlstm_C, state_mlstm_n, state_mlstm_m, state_ssm, ffn1_norm, ffn1_w_gate, ffn1_w_up, ffn1_w_down, mix_norm, w_in, ml_i_bias, ml_f_bias, ml_head_norm, ssm_conv_w, ssm_conv_b, ssm_dt_bias, ssm_A_log, ssm_D, ssm_norm, w_out, ffn2_norm, ffn2_w_gate, ffn2_w_up, ffn2_w_down, final_norm):
    depth = w_in.shape[0]
    assert depth == 1, "the final norm is fused into the (single) layer's second FFN"
    bp, seq, d = x_prompt.shape
    bs = x_sample.shape[0]
    l = 0
    tm = 1024
    assert x_sample.shape[1] == 1 and seq % CHUNK == 0 and seq % tm == 0 and bs % SUBLANES == 0

    head_of_lane = jnp.arange(SSM_WIDTH) // SSM_HEADDIM
    w = {
        "bias_row": (_pad_lanes(ssm_dt_bias[l], SM_DT) + _pad_lanes(ml_i_bias[l], SM_IG)
                     + _pad_lanes(ml_f_bias[l], SM_FG)).reshape(1, LANES),
        "alog_row": _pad_lanes(ssm_A_log[l], SM_DT).reshape(1, LANES),
        "ml_head_norm": ml_head_norm[l], "ssm_conv_w": ssm_conv_w[l],
        "ssm_conv_b": ssm_conv_b[l].reshape(1, XBC_DIM),
        "dskip_row": jnp.repeat(ssm_D[l].astype(F32), SSM_HEADDIM).reshape(1, SSM_WIDTH),
        "ssm_norm": ssm_norm[l],
        "e1": (jnp.arange(LANES)[:, None] == head_of_lane[None, :]).astype(BF16),
    }

    xp = x_prompt.reshape(bp * seq, d)
    xs = x_sample.reshape(bs, d)
    xp1, xs1 = _ffn(xp, xs, ffn1_norm[l], ffn1_w_gate[l], ffn1_w_up[l], ffn1_w_down[l], final_norm,
                    final_norm=False, tm=tm, tf=256)
    w_in_r = _regroup_w_in(jnp.swapaxes(w_in[l], 0, 1), tc=256)
    proj_p, proj_s, p_tail = _inproj(xp1, xs1, mix_norm[l], w_in_r, _activation_rows(), w["ssm_conv_w"],
                                     w["ssm_conv_b"], tm=tm, tn=PROJ_TILE, rows_per_seq=seq)
    p_conv = p_tail[seq // tm - 1::seq // tm, SUBLANES - HIST:, :]

    mp, p_c, p_n, p_m, p_s = _mixer_chunk(proj_p, w, n_seq=bp, n_chunks=seq // CHUNK)
    ms, s_conv, s_n, s_m, s_c, s_s = _mixer_step(
        proj_s, state_conv[l].reshape(bs, HIST * XBC_DIM), state_mlstm_n[l].reshape(bs, ML_QK),
        _pad_lanes(state_mlstm_m[l], SM_IG), state_mlstm_C[l],
        state_ssm[l].reshape(bs, SSM_WIDTH, SSM_STATE), w, seq_per_step=2)

    xp2, xs2 = _outproj(mp, ms, w_out[l], xp1, xs1, tm=2 * tm, tn=512)
    yp, ys = _ffn(xp2, xs2, ffn2_norm[l], ffn2_w_gate[l], ffn2_w_up[l], ffn2_w_down[l], final_norm,
                  final_norm=True, tm=tm, tf=256)

    ssm_shape = (SSM_HEADS, SSM_HEADDIM, SSM_STATE)
    return (yp.reshape(bp, seq, d), ys.reshape(bs, 1, d),
            p_conv[None], p_c[None], p_n[None], p_m[:, 0, :ML_HEADS][None], p_s.reshape(bp, *ssm_shape)[None],
            s_conv.reshape(bs, HIST, XBC_DIM)[None], s_c[None], s_n.reshape(bs, ML_HEADS, ML_DK)[None],
            s_m[:, SM_IG:SM_IG + ML_HEADS][None], s_s.reshape(bs, *ssm_shape)[None])
```

```python
import functools

import numpy as np
import jax
import jax.numpy as jnp
from jax import lax
from jax.experimental import pallas as pl
from jax.experimental.pallas import tpu as pltpu

F32 = jnp.float32
BF16 = jnp.bfloat16

EPS = 1e-6
NEG_INIT = -1e30

D_MODEL = 2048
ML_HEADS = 4
ML_DV = 512
ML_DK = 256
ML_QK = ML_HEADS * ML_DK
SSM_HEADS = 32
SSM_HEADDIM = 64
SSM_GROUPS = 2
SSM_STATE = 128
SSM_WIDTH = SSM_HEADS * SSM_HEADDIM
CONV_W = 4
HIST = CONV_W - 1
BC_DIM = 2 * SSM_GROUPS * SSM_STATE
XBC_DIM = SSM_WIDTH + BC_DIM
CHUNK = 128
IN_SIZES = (ML_QK, ML_QK, D_MODEL, ML_HEADS, ML_HEADS, D_MODEL, SSM_WIDTH, XBC_DIM, SSM_HEADS, 2 * D_MODEL)

LANES = 128
SUBLANES = 8
MXU_WIDTH = 256

SEG_XS = 0
SEG_BC = SEG_XS + SSM_WIDTH
SEG_GATES = SEG_BC + BC_DIM
SEG_OG = SEG_GATES + 2 * D_MODEL
SEG_Z = SEG_OG + D_MODEL
SEG_V = SEG_Z + SSM_WIDTH
SEG_Q = SEG_V + D_MODEL
SEG_K = SEG_Q + ML_QK
SEG_SMALL = SEG_K + ML_QK
PROJ_TILE = 1280
PROJ_DIM = -(-(SEG_SMALL + LANES) // PROJ_TILE) * PROJ_TILE
assert SEG_XS == 0 and XBC_DIM % PROJ_TILE == 0
CONV_TILES = XBC_DIM // PROJ_TILE
ACT_TILES = -(-SEG_V // PROJ_TILE)
SM_DT = 0
SM_IG = SM_DT + SSM_HEADS
SM_FG = SM_IG + ML_HEADS

VMEM_LIMIT = 60 * 1024 * 1024


def _rms(x):
    return x * lax.rsqrt(jnp.mean(x * x, axis=-1, keepdims=True) + EPS)


def _sigmoid(x):
    return 0.5 * jnp.tanh(0.5 * x) + 0.5


def _silu(x):
    h = 0.5 * x
    return h * jnp.tanh(h) + h


def _softplus(x):
    return jnp.maximum(x, 0.0) + jnp.log(1.0 + jnp.exp(-jnp.abs(x)))


def _dot(a, b):
    return jnp.dot(a, b, preferred_element_type=F32)


def _dot_nt(a, b):
    return lax.dot_general(a, b, (((1,), (1,)), ((), ())), preferred_element_type=F32)


def _dot_tn(a, b):
    return lax.dot_general(a, b, (((0,), (0,)), ((), ())), preferred_element_type=F32)


def _split3(x):
    hi = x.astype(BF16)
    r1 = x - hi.astype(F32)
    mid = r1.astype(BF16)
    lo = (r1 - mid.astype(F32)).astype(BF16)
    return hi, mid, lo


def _expand_heads(x, e2):
    x_hi, x_mid, x_lo = _split3(x)
    return _dot(jnp.concatenate([x_hi, x_mid], axis=1), e2) + _dot(x_lo, e2[0:LANES, :])


def _params(dims=2):
    return pltpu.CompilerParams(dimension_semantics=("arbitrary",) * dims, vmem_limit_bytes=VMEM_LIMIT)


def _ffn_kernel(xp_ref, xs_ref, nw_ref, wg_ref, wu_ref, wd_ref, fw_ref, op_ref, os_ref, hp_ref, hs_ref,
                *, final_norm):
    i = pl.program_id(0)
    j = pl.program_id(1)
    last_j = pl.num_programs(1) - 1

    def rows(x_ref, o_ref, h_ref):
        @pl.when(j == 0)
        def _():
            h_ref[...] = (_rms(x_ref[...]) * nw_ref[...]).astype(BF16)
            o_ref[...] = jnp.zeros(o_ref.shape, F32)

        h = h_ref[...]
        g = _dot(h, wg_ref[...].astype(BF16))
        u = _dot(h, wu_ref[...].astype(BF16))
        a = (_silu(g) * u).astype(BF16)
        o_ref[...] += _dot(a, wd_ref[...].astype(BF16))

        @pl.when(j == last_j)
        def _():
            strip = min(x_ref.shape[0], LANES)

            def finish(r, carry):
                rows_ = pl.ds(pl.multiple_of(r * strip, strip), strip)
                y = x_ref[rows_, :] + 0.5 * o_ref[rows_, :]
                if final_norm:
                    y = _rms(y) * fw_ref[...]
                o_ref[rows_, :] = y
                return carry

            lax.fori_loop(0, x_ref.shape[0] // strip, finish, 0)

    rows(xp_ref, op_ref, hp_ref)

    @pl.when(i == pl.num_programs(0) - 1)
    def _():
        rows(xs_ref, os_ref, hs_ref)


def _ffn(xp, xs, norm_w, w_gate, w_up, w_down, final_w, *, final_norm, tm, tf):
    mp, d = xp.shape
    ms = xs.shape[0]
    f = w_gate.shape[1]
    return pl.pallas_call(
        functools.partial(_ffn_kernel, final_norm=final_norm),
        grid=(mp // tm, f // tf),
        in_specs=[
            pl.BlockSpec((tm, d), lambda i, j: (i, 0)),
            pl.BlockSpec((ms, d), lambda i, j: (0, 0), pipeline_mode=pl.Buffered(1)),
            pl.BlockSpec((1, d), lambda i, j: (0, 0)),
            pl.BlockSpec((d, tf), lambda i, j: (0, j)),
            pl.BlockSpec((d, tf), lambda i, j: (0, j)),
            pl.BlockSpec((tf, d), lambda i, j: (j, 0)),
            pl.BlockSpec((1, d), lambda i, j: (0, 0)),
        ],
        out_specs=[pl.BlockSpec((tm, d), lambda i, j: (i, 0)),
                   pl.BlockSpec((ms, d), lambda i, j: (0, 0))],
        out_shape=[jax.ShapeDtypeStruct((mp, d), F32), jax.ShapeDtypeStruct((ms, d), F32)],
        scratch_shapes=[pltpu.VMEM((tm, d), BF16), pltpu.VMEM((ms, d), BF16)],
        compiler_params=_params(),
        name="ffn",
    )(xp, xs, norm_w.reshape(1, d), w_gate, w_up, w_down, final_w.reshape(1, d))


def _causal_conv(cur, prev, w_ref, b_ref):
    row8 = lax.broadcasted_iota(jnp.int32, (SUBLANES, 1), 0)
    acc = cur * w_ref[HIST:HIST + 1, :] + b_ref[...]
    for back in range(1, CONV_W):
        sh = pltpu.roll(cur, back, axis=0)
        head = jnp.where(row8 < back, pltpu.roll(prev, back, axis=0), sh[0:SUBLANES])
        sh = jnp.concatenate([head, sh[SUBLANES:]], axis=0)
        acc = acc + sh * w_ref[HIST - back:HIST - back + 1, :]
    return acc


def _inproj_kernel(xp_ref, xs_ref, nw_ref, w_ref, act_ref, cw_ref, cb_ref, op_ref, os_ref, tail_ref,
                   up_ref, us_ref, hist_ref, raw_ref, *, blocks_per_seq):
    i = pl.program_id(0)
    j = pl.program_id(1)
    tm = xp_ref.shape[0]

    @pl.when(j == 0)
    def _():
        up_ref[...] = (_rms(xp_ref[...]) * nw_ref[...]).astype(BF16)

    @pl.when((i == 0) & (j == 0))
    def _():
        hist_ref[...] = jnp.zeros(hist_ref.shape, F32)

    def activate(x, cols):
        return ((act_ref[0:1, cols] + act_ref[1:2, cols] * x) * _sigmoid(x)
                + act_ref[2:3, cols] * x)

    is_conv = j < CONV_TILES
    is_act = (j >= CONV_TILES) & (j < ACT_TILES)
    strips = [slice(s, s + MXU_WIDTH) for s in range(0, w_ref.shape[0], MXU_WIDTH)]

    @pl.when(is_conv)
    def _():
        jj = jnp.minimum(j, CONV_TILES - 1)
        fresh = i % blocks_per_seq == 0
        for n, cols in enumerate(strips):
            raw_ref[n % 2] = _dot_nt(up_ref[...], w_ref[cols, :])
            raw = raw_ref[n % 2]
            prev = jnp.where(fresh, 0.0, hist_ref[jj, :, cols])
            tail = raw[tm - SUBLANES:tm]
            hist_ref[jj, :, cols] = tail
            tail_ref[0, :, cols] = tail
            op_ref[:, cols] = _silu(_causal_conv(raw, prev, cw_ref.at[:, cols], cb_ref.at[:, cols]))

    @pl.when(is_act)
    def _():
        for n, cols in enumerate(strips):
            raw_ref[n % 2] = _dot_nt(up_ref[...], w_ref[cols, :])
            op_ref[:, cols] = activate(raw_ref[n % 2], cols)

    @pl.when(j >= ACT_TILES)
    def _():
        op_ref[...] = _dot_nt(up_ref[...], w_ref[...])

    @pl.when(i == pl.num_programs(0) - 1)
    def _():
        @pl.when(j == 0)
        def _():
            us_ref[...] = (_rms(xs_ref[...]) * nw_ref[...]).astype(BF16)

        raw = _dot_nt(us_ref[...], w_ref[...])
        os_ref[...] = jnp.where(is_act, activate(raw, slice(None)), raw)


def _inproj(xp, xs, norm_w, wt, act, conv_w, conv_b, *, tm, tn, rows_per_seq):
    mp, d = xp.shape
    ms = xs.shape[0]
    n = wt.shape[0]
    last_i = mp // tm - 1
    blocks_per_seq = rows_per_seq // tm

    def conv_tile(i, j):
        return (0, jnp.minimum(j, CONV_TILES - 1))

    return pl.pallas_call(
        functools.partial(_inproj_kernel, blocks_per_seq=blocks_per_seq),
        grid=(mp // tm, n // tn),
        in_specs=[
            pl.BlockSpec((tm, d), lambda i, j: (i, 0)),
            pl.BlockSpec((ms, d), lambda i, j: (0, 0), pipeline_mode=pl.Buffered(1)),
            pl.BlockSpec((1, d), lambda i, j: (0, 0)),
            pl.BlockSpec((tn, d), lambda i, j: (j, 0)),
            pl.BlockSpec((SUBLANES, tn), lambda i, j: (0, j)),
            pl.BlockSpec((CONV_W, tn), conv_tile),
            pl.BlockSpec((1, tn), conv_tile),
        ],
        out_specs=[pl.BlockSpec((tm, tn), lambda i, j: (i, j)),
                   pl.BlockSpec((ms, tn), lambda i, j: (0, jnp.where(i == last_i, j, 0))),
                   pl.BlockSpec((1, SUBLANES, tn), lambda i, j: (i, 0, jnp.minimum(j, CONV_TILES - 1)))],
        out_shape=[jax.ShapeDtypeStruct((mp, n), F32), jax.ShapeDtypeStruct((ms, n), F32),
                   jax.ShapeDtypeStruct((mp // tm, SUBLANES, XBC_DIM), F32)],
        scratch_shapes=[pltpu.VMEM((tm, d), BF16), pltpu.VMEM((ms, d), BF16),
                        pltpu.VMEM((CONV_TILES, SUBLANES, tn), F32),
                        pltpu.VMEM((2, tm, MXU_WIDTH), F32)],
        compiler_params=_params(),
        name="inproj",
    )(xp, xs, norm_w.reshape(1, d), wt, act, conv_w, conv_b)


def _outproj_kernel(yp_ref, ys_ref, w_ref, xp_ref, xs_ref, op_ref, os_ref):
    w = w_ref[...].astype(BF16)
    op_ref[...] = xp_ref[...] + _dot(yp_ref[...].astype(BF16), w)

    @pl.when(pl.program_id(0) == pl.num_programs(0) - 1)
    def _():
        os_ref[...] = xs_ref[...] + _dot(ys_ref[...].astype(BF16), w)


def _outproj(yp, ys, w, xp, xs, *, tm, tn):
    mp, d = yp.shape
    ms = ys.shape[0]
    n = w.shape[1]
    last_i = mp // tm - 1

    def sample_tile(i, j):
        return (0, jnp.where(i == last_i, j, 0))

    return pl.pallas_call(
        _outproj_kernel,
        grid=(mp // tm, n // tn),
        in_specs=[
            pl.BlockSpec((tm, d), lambda i, j: (i, 0)),
            pl.BlockSpec((ms, d), lambda i, j: (0, 0)),
            pl.BlockSpec((d, tn), lambda i, j: (0, j)),
            pl.BlockSpec((tm, tn), lambda i, j: (i, j)),
            pl.BlockSpec((ms, tn), sample_tile),
        ],
        out_specs=[pl.BlockSpec((tm, tn), lambda i, j: (i, j)),
                   pl.BlockSpec((ms, tn), sample_tile)],
        out_shape=[jax.ShapeDtypeStruct((mp, n), F32), jax.ShapeDtypeStruct((ms, n), F32)],
        compiler_params=_params(),
        name="outproj",
    )(yp, ys, w, xp, xs)


def _seg_sources():
    offs = [0]
    for s in IN_SIZES:
        offs.append(offs[-1] + s)
    q, k, v, ig, fg, og, z, xbc, dtr, gates = offs[:-1]
    assert fg == ig + ML_HEADS and SM_FG == SM_IG + ML_HEADS
    return [(SEG_GATES, gates, 2 * D_MODEL), (SEG_OG, og, D_MODEL), (SEG_Z, z, SSM_WIDTH),
            (SEG_V, v, D_MODEL), (SEG_XS, xbc, SSM_WIDTH), (SEG_Q, q, ML_QK), (SEG_K, k, ML_QK),
            (SEG_BC, xbc + SSM_WIDTH, BC_DIM), (SEG_SMALL + SM_DT, dtr, SSM_HEADS),
            (SEG_SMALL + SM_IG, ig, 2 * ML_HEADS)]


def _activation_rows():
    col = np.arange(PROJ_DIM)
    sig = (col >= SEG_GATES) & (col < SEG_Z)
    silu = (col >= SEG_Z) & (col < SEG_V)
    rows = np.zeros((SUBLANES, PROJ_DIM), np.float32)
    rows[0], rows[1], rows[2] = sig, silu, ~(sig | silu)
    return jnp.asarray(rows)


def _regroup_kernel(wt_ref, o_ref):
    cols = wt_ref.shape[1]
    small = []
    for dst, src, width in _seg_sources():
        assert src % SUBLANES == 0 and width % SUBLANES == 0
        if dst < SEG_SMALL:
            o_ref[dst:dst + width, :] = wt_ref[src:src + width, :].astype(BF16)
        else:
            assert dst - SEG_SMALL == sum(p.shape[0] for p in small)
            small.append(wt_ref[src:src + width, :])
    used = sum(p.shape[0] for p in small)
    small.append(jnp.zeros((PROJ_DIM - SEG_SMALL - used, cols), F32))
    o_ref[SEG_SMALL:PROJ_DIM, :] = jnp.concatenate(small, axis=0).astype(BF16)


def _regroup_w_in(wt, *, tc):
    n_in, d = wt.shape
    return pl.pallas_call(
        _regroup_kernel,
        grid=(d // tc,),
        in_specs=[pl.BlockSpec((n_in, tc), lambda i: (0, i))],
        out_specs=pl.BlockSpec((PROJ_DIM, tc), lambda i: (0, i)),
        out_shape=jax.ShapeDtypeStruct((PROJ_DIM, d), BF16),
        compiler_params=_params(1),
        name="regroup",
    )(wt)


def _mixer_kernel(proj_ref, bias_ref, alog_ref, dskip_ref, hnorm_ref, snorm_ref, e2_ref,
                  merged_ref, c_ref, n_ref, m_ref, s_ref):
    c = pl.program_id(1)

    @pl.when(c == 0)
    def _():
        c_ref[...] = jnp.zeros(c_ref.shape, F32)
        n_ref[...] = jnp.zeros(n_ref.shape, F32)
        m_ref[...] = jnp.full(m_ref.shape, NEG_INIT, F32)
        s_ref[...] = jnp.zeros(s_ref.shape, F32)

    for p in range(proj_ref.shape[0]):
        _mixer_sequence(proj_ref.at[p], bias_ref, alog_ref, dskip_ref, hnorm_ref, snorm_ref, e2_ref,
                        merged_ref.at[p], c_ref.at[p], n_ref.at[p], m_ref.at[p], s_ref.at[p])


def _mixer_sequence(proj_ref, bias_ref, alog_ref, dskip_ref, hnorm_ref, snorm_ref, e2_ref,
                    merged_ref, c_ref, n_ref, m_ref, s_ref):
    xs_act = proj_ref[:, SEG_XS:SEG_XS + SSM_WIDTH]
    bc_act = proj_ref[:, SEG_BC:SEG_BC + BC_DIM]

    lane = lax.broadcasted_iota(jnp.int32, (CHUNK, LANES), 1)
    row = lax.broadcasted_iota(jnp.int32, (CHUNK, LANES), 0)
    lane1 = lax.broadcasted_iota(jnp.int32, (1, LANES), 1)
    is_dt = lane < SM_IG
    is_li = (lane >= SM_IG) & (lane < SM_FG)
    is_lf = (lane >= SM_FG) & (lane < SM_FG + ML_HEADS)
    pre = proj_ref[:, SEG_SMALL:SEG_SMALL + LANES] + bias_ref[...]
    dt = jnp.where(is_dt, _softplus(pre), 0.0)
    lf = jnp.where(is_lf, -_softplus(-pre), 0.0)
    li = jnp.where(is_li, pre, 0.0)
    a_row = jnp.where(lane1 < SM_IG, -jnp.exp(alog_ref[...]), 0.0)
    causal = row >= lane
    tril = jnp.where(causal, 1.0, 0.0).astype(BF16)
    hi, mid, lo = _split3(dt * a_row + lf)
    cs = _dot(tril, hi) + _dot(tril, mid) + _dot(tril, lo)
    cs_t = cs.T
    li_t = li.T
    dt_t = dt.T
    b_last = cs[CHUNK - 1:CHUNK, :]
    ecs = jnp.exp(cs)

    q_all = (proj_ref[:, SEG_Q:SEG_Q + ML_QK] * (ML_DK ** -0.5)).astype(BF16)
    k_all = proj_ref[:, SEG_K:SEG_K + ML_QK]
    v_all = proj_ref[:, SEG_V:SEG_V + D_MODEL].astype(BF16)
    og_gate = proj_ref[:, SEG_OG:SEG_OG + D_MODEL]
    m_old = m_ref[...]
    m_new_row = m_old
    y_a_parts = []
    for h in range(ML_HEADS):
        ck = SM_FG + h
        b_col = cs[:, ck:ck + 1]
        b_row = cs_t[ck:ck + 1, :]
        li_row = li_t[SM_IG + h:SM_IG + h + 1, :]
        li_col = li[:, SM_IG + h:SM_IG + h + 1]
        m_prev = m_old[:, h:h + 1]
        dmat = jnp.where(causal, b_col - b_row + li_row, -jnp.inf)
        inter = b_col + m_prev
        m_t = jnp.maximum(inter, jnp.max(dmat, axis=1, keepdims=True))
        w_intra = jnp.exp(dmat - m_t)
        w_inter = jnp.exp(inter - m_t)
        qh = q_all[:, h * ML_DK:(h + 1) * ML_DK]
        kh = k_all[:, h * ML_DK:(h + 1) * ML_DK]
        vh = v_all[:, h * ML_DV:(h + 1) * ML_DV]
        c_old = c_ref[h]
        n_old = n_ref[h:h + 1, :]
        s = _dot_nt(qh, kh.astype(BF16)) * w_intra
        num = _dot(s.astype(BF16), vh) + w_inter * _dot(qh, c_old.astype(BF16))
        qn = jnp.sum(qh.astype(F32) * n_old, axis=1, keepdims=True)
        den = jnp.sum(s, axis=1, keepdims=True) + w_inter * qn
        hh = num / jnp.maximum(jnp.abs(den), jnp.exp(-m_t))
        m_new = m_t[CHUNK - 1:CHUNK, :]
        bl = b_col[CHUNK - 1:CHUNK, :]
        w_end = jnp.exp(bl - b_col + li_col - m_new)
        decay = jnp.exp(bl + m_prev - m_new)
        kw = kh * w_end
        c_ref[h] = decay * c_old + _dot_tn(kw.astype(BF16), vh)
        n_ref[h:h + 1, :] = decay * n_old + jnp.sum(kw, axis=0, keepdims=True)
        m_new_row = jnp.where(lane1 == h, m_new, m_new_row)
        hn = _rms(hh) * hnorm_ref[:, h * ML_DV:(h + 1) * ML_DV]
        y_a_parts.append(og_gate[:, h * ML_DV:(h + 1) * ML_DV] * hn)
    m_ref[...] = m_new_row
    y_a = jnp.concatenate(y_a_parts, axis=1)

    e2 = e2_ref[...]
    e_b = _expand_heads(jnp.where(is_dt, ecs, 0.0), e2)
    e_w = _expand_heads(jnp.exp(b_last - cs) * dt, e2)
    xs_bf = xs_act.astype(BF16)
    xw_bf = (xs_act * e_w).astype(BF16)
    half = lane < SSM_HEADDIM
    gw = SSM_WIDTH // SSM_GROUPS
    hpg = SSM_HEADS // SSM_GROUPS
    y_parts = []
    for g in range(SSM_GROUPS):
        bg = bc_act[:, g * SSM_STATE:(g + 1) * SSM_STATE].astype(BF16)
        cg = bc_act[:, (SSM_GROUPS + g) * SSM_STATE:(SSM_GROUPS + g + 1) * SSM_STATE].astype(BF16)
        cb = _dot_nt(cg, bg)
        s_old = s_ref[g * gw:(g + 1) * gw, :]
        y_inter = _dot_nt(cg, s_old.astype(BF16))
        pairs = []
        for j in range(hpg // 2):
            x_pair = xs_bf[:, g * gw + j * LANES:g * gw + (j + 1) * LANES]
            res = []
            for hh_ in range(2):
                h = g * hpg + 2 * j + hh_
                dec = jnp.exp(jnp.where(causal, cs[:, h:h + 1] - cs_t[h:h + 1, :], -jnp.inf))
                mm = (cb * dec * dt_t[h:h + 1, :]).astype(BF16)
                res.append(_dot(mm, x_pair))
            pairs.append(jnp.where(half, res[0], res[1]))
        y_intra = jnp.concatenate(pairs, axis=1)
        y_parts.append(y_intra + e_b[:, g * gw:(g + 1) * gw] * y_inter)
        upd = _dot_tn(xw_bf[:, g * gw:(g + 1) * gw], bg)
        for hh_ in range(hpg):
            h = g * hpg + hh_
            lo_r = g * gw + hh_ * SSM_HEADDIM
            s_ref[lo_r:lo_r + SSM_HEADDIM, :] = (
                ecs[CHUNK - 1:CHUNK, h:h + 1] * s_ref[lo_r:lo_r + SSM_HEADDIM, :]
                + upd[hh_ * SSM_HEADDIM:(hh_ + 1) * SSM_HEADDIM, :])
    y = jnp.concatenate(y_parts, axis=1)
    y_s = (y + dskip_ref[...] * xs_act) * proj_ref[:, SEG_Z:SEG_Z + SSM_WIDTH]
    y_b = jnp.concatenate(
        [_rms(y_s[:, g * gw:(g + 1) * gw]) for g in range(SSM_GROUPS)], axis=1) * snorm_ref[...]

    merged_ref[...] = (proj_ref[:, SEG_GATES:SEG_GATES + D_MODEL] * y_a
                       + proj_ref[:, SEG_GATES + D_MODEL:SEG_GATES + 2 * D_MODEL] * y_b
                       ).astype(merged_ref.dtype)


def _const_spec(shape, single_buffer=False):
    kwargs = dict(pipeline_mode=pl.Buffered(1)) if single_buffer else {}
    return pl.BlockSpec(shape, lambda *_: (0,) * len(shape), **kwargs)


def _mixer_params(w):
    return [w["bias_row"], w["alog_row"], w["dskip_row"], w["ml_head_norm"].reshape(1, D_MODEL),
            w["ssm_norm"].reshape(1, SSM_WIDTH), w["e2"]]


_MIXER_PARAM_SHAPES = [(1, LANES), (1, LANES), (1, SSM_WIDTH), (1, D_MODEL), (1, SSM_WIDTH),
                       (2 * LANES, SSM_WIDTH)]


def _mixer_chunk(proj, w, *, n_seq, n_chunks, seq_per_step):
    def per_seq(shape):
        return pl.BlockSpec((seq_per_step,) + shape, lambda b, c: (b,) + (0,) * len(shape))

    state_shapes = [(ML_HEADS, ML_DK, ML_DV), (ML_HEADS, ML_DK), (1, LANES), (SSM_WIDTH, SSM_STATE)]
    in_specs = [pl.BlockSpec((seq_per_step, CHUNK, PROJ_DIM), lambda b, c: (b, c, 0))] + [
        _const_spec(s) for s in _MIXER_PARAM_SHAPES]
    out_specs = [pl.BlockSpec((seq_per_step, CHUNK, D_MODEL), lambda b, c: (b, c, 0))] + [
        per_seq(s) for s in state_shapes]
    out_shape = [jax.ShapeDtypeStruct((n_seq, n_chunks * CHUNK, D_MODEL), BF16)] + [
        jax.ShapeDtypeStruct((n_seq,) + s, F32) for s in state_shapes]
    merged, *states = pl.pallas_call(
        _mixer_kernel,
        grid=(n_seq // seq_per_step, n_chunks),
        in_specs=in_specs,
        out_specs=out_specs,
        out_shape=out_shape,
        compiler_params=_params(),
        name="mixer_chunk",
    )(proj.reshape(n_seq, n_chunks * CHUNK, PROJ_DIM), *_mixer_params(w))
    return (merged.reshape(n_seq * n_chunks * CHUNK, D_MODEL), *states)


def _step_kernel(proj_ref, conv0_ref, n0_ref, m0_ref, c0_ref, s0_ref,
                 bias_ref, alog_ref, dskip_ref, hnorm_ref, snorm_ref, e2_ref, convw_ref, convb_ref,
                 merged_ref, conv_ref, n_ref, m_ref, c_ref, s_ref,
                 qf_ref, kt_ref, vb_ref, cmf_ref, bb_ref, xwt_ref, dec_ref, accq_ref, accs_ref):
    b = pl.program_id(0)
    ns = proj_ref.shape[0]
    gw = SSM_WIDTH // SSM_GROUPS
    hpg = SSM_HEADS // SSM_GROUPS
    lane = lax.broadcasted_iota(jnp.int32, (ns, LANES), 1)
    lane1 = lax.broadcasted_iota(jnp.int32, (1, LANES), 1)
    is_dt = lane < SM_IG
    is_li = (lane >= SM_IG) & (lane < SM_FG)
    is_lf = (lane >= SM_FG) & (lane < SM_FG + ML_HEADS)

    def gate_scalars():
        pre = proj_ref[:, SEG_SMALL:SEG_SMALL + LANES] + bias_ref[...]
        dt = jnp.where(is_dt, _softplus(pre), 0.0)
        lf = pltpu.roll(jnp.where(is_lf, -_softplus(-pre), 0.0), LANES - ML_HEADS, axis=1)
        li = jnp.where(is_li, pre, 0.0)
        inter = lf + m0_ref[...]
        m_t = jnp.maximum(inter, li)
        a_row = jnp.where(lane1 < SM_IG, -jnp.exp(alog_ref[...]), 0.0)
        return dict(dt=dt, m_t=m_t, w_intra=jnp.exp(li - m_t), w_inter=jnp.exp(inter - m_t),
                    e=jnp.exp(dt * a_row))

    def conv_act(lo, hi, seg, store):
        x_new = proj_ref[:, seg:seg + hi - lo]
        st = [conv0_ref[:, j * XBC_DIM + lo:j * XBC_DIM + hi] for j in range(HIST)]
        acc = st[0] * convw_ref[0:1, lo:hi]
        for j in range(1, HIST):
            acc = acc + st[j] * convw_ref[j:j + 1, lo:hi]
        acc = acc + x_new * convw_ref[HIST:HIST + 1, lo:hi]
        if store:
            for j in range(1, HIST):
                conv_ref[j - 1, :, lo:hi] = st[j]
            conv_ref[HIST - 1, :, lo:hi] = x_new
        return _silu(acc + convb_ref[:, lo:hi])

    def head_col(x, h):
        return x[:, SM_IG + h:SM_IG + h + 1]

    @pl.when(b == 0)
    def _():
        gs = gate_scalars()
        m_ref[...] = gs["m_t"]
        dec_ref[...] = jnp.where(is_dt, gs["e"], gs["w_inter"])
        qf_ref[...] = (proj_ref[:, SEG_Q:SEG_Q + ML_QK] * (ML_DK ** -0.5)).astype(BF16).astype(F32)
        vb_ref[...] = proj_ref[:, SEG_V:SEG_V + D_MODEL].astype(BF16)
        for h in range(ML_HEADS):
            sl = slice(h * ML_DK, (h + 1) * ML_DK)
            kw = proj_ref[:, SEG_K + h * ML_DK:SEG_K + (h + 1) * ML_DK] * head_col(gs["w_intra"], h)
            n_ref[:, sl] = head_col(gs["w_inter"], h) * n0_ref[:, sl] + kw
            kt_ref[sl, :] = kw.T
        xs_act = conv_act(0, SSM_WIDTH, SEG_XS, True)
        bc_act = conv_act(SSM_WIDTH, XBC_DIM, SEG_BC, True)
        bb_ref[...] = bc_act[:, 0:SSM_GROUPS * SSM_STATE].astype(BF16)
        cmf_ref[...] = bc_act[:, SSM_GROUPS * SSM_STATE:BC_DIM]
        xw = xs_act * _expand_heads(gs["dt"], e2_ref[...])
        for g in range(SSM_GROUPS):
            xwt_ref[g * gw:(g + 1) * gw, :] = xw[:, g * gw:(g + 1) * gw].T
        accq_ref[...] = jnp.zeros(accq_ref.shape, F32)
        accs_ref[...] = jnp.zeros(accs_ref.shape, F32)

    for sb in range(c0_ref.shape[0]):
        seq = b * c0_ref.shape[0] + sb
        base = pl.multiple_of((seq // SUBLANES) * SUBLANES, SUBLANES)
        is_row = lax.broadcasted_iota(jnp.int32, (SUBLANES, 1), 0) == seq % SUBLANES
        is_seq = lax.broadcasted_iota(jnp.int32, (1, ns), 1) == seq
        q8 = jnp.where(is_row, qf_ref[pl.ds(base, SUBLANES), :], 0.0).astype(BF16)
        c8 = jnp.where(is_row, cmf_ref[pl.ds(base, SUBLANES), :], 0.0).astype(BF16)
        drow = dec_ref[pl.ds(seq, 1), :]
        for h in range(ML_HEADS):
            c_old = c0_ref[sb, h]
            accq_ref[pl.ds(base, SUBLANES), h * ML_DV:(h + 1) * ML_DV] += _dot(
                q8[:, h * ML_DK:(h + 1) * ML_DK], c_old.astype(BF16))
            kt = jnp.where(is_seq, kt_ref[h * ML_DK:(h + 1) * ML_DK, :], 0.0).astype(BF16)
            c_ref[sb, h] = (drow[:, SM_IG + h:SM_IG + h + 1] * c_old
                            + _dot(kt, vb_ref[:, h * ML_DV:(h + 1) * ML_DV]))
        for g in range(SSM_GROUPS):
            s_old = s0_ref[sb, g * gw:(g + 1) * gw, :]
            accs_ref[pl.ds(base, SUBLANES), g * gw:(g + 1) * gw] += _dot_nt(
                c8[:, g * SSM_STATE:(g + 1) * SSM_STATE], s_old.astype(BF16))
            xwt = jnp.where(is_seq, xwt_ref[g * gw:(g + 1) * gw, :], 0.0).astype(BF16)
            upd = _dot(xwt, bb_ref[:, g * SSM_STATE:(g + 1) * SSM_STATE])
            for hh_ in range(hpg):
                h = g * hpg + hh_
                rs = slice(hh_ * SSM_HEADDIM, (hh_ + 1) * SSM_HEADDIM)
                s_ref[sb, g * gw + hh_ * SSM_HEADDIM:g * gw + (hh_ + 1) * SSM_HEADDIM, :] = (
                    drow[:, h:h + 1] * s_old[rs, :] + upd[rs, :])

    @pl.when(b == pl.num_programs(0) - 1)
    def _():
        gs = gate_scalars()
        q = qf_ref[...]
        y_a_parts = []
        for h in range(ML_HEADS):
            qh = q[:, h * ML_DK:(h + 1) * ML_DK]
            kh = proj_ref[:, SEG_K + h * ML_DK:SEG_K + (h + 1) * ML_DK].astype(BF16).astype(F32)
            vh = vb_ref[:, h * ML_DV:(h + 1) * ML_DV].astype(F32)
            w_inter = head_col(gs["w_inter"], h)
            s = jnp.sum(qh * kh, axis=1, keepdims=True) * head_col(gs["w_intra"], h)
            qn = jnp.sum(qh * n0_ref[:, h * ML_DK:(h + 1) * ML_DK], axis=1, keepdims=True)
            num = s.astype(BF16).astype(F32) * vh + w_inter * accq_ref[:, h * ML_DV:(h + 1) * ML_DV]
            den = s + w_inter * qn
            hh = num / jnp.maximum(jnp.abs(den), jnp.exp(-head_col(gs["m_t"], h)))
            hn = _rms(hh) * hnorm_ref[:, h * ML_DV:(h + 1) * ML_DV]
            y_a_parts.append(proj_ref[:, SEG_OG + h * ML_DV:SEG_OG + (h + 1) * ML_DV] * hn)
        y_a = jnp.concatenate(y_a_parts, axis=1)

        xs_act = conv_act(0, SSM_WIDTH, SEG_XS, False)
        bb = bb_ref[...].astype(F32)
        cm = cmf_ref[...].astype(BF16).astype(F32)
        e2 = e2_ref[...]
        x_dt = xs_act * _expand_heads(gs["dt"], e2)
        e_e = _expand_heads(jnp.where(is_dt, gs["e"], 0.0), e2)
        y_parts = []
        for g in range(SSM_GROUPS):
            gl = slice(g * SSM_STATE, (g + 1) * SSM_STATE)
            cb = jnp.sum(cm[:, gl] * bb[:, gl], axis=1, keepdims=True)
            cl = slice(g * gw, (g + 1) * gw)
            y_parts.append(cb * x_dt[:, cl] + e_e[:, cl] * accs_ref[:, cl])
        y = jnp.concatenate(y_parts, axis=1)
        y_s = (y + dskip_ref[...] * xs_act) * proj_ref[:, SEG_Z:SEG_Z + SSM_WIDTH]
        y_b = jnp.concatenate(
            [_rms(y_s[:, g * gw:(g + 1) * gw]) for g in range(SSM_GROUPS)], axis=1) * snorm_ref[...]
        merged_ref[...] = (proj_ref[:, SEG_GATES:SEG_GATES + D_MODEL] * y_a
                           + proj_ref[:, SEG_GATES + D_MODEL:SEG_GATES + 2 * D_MODEL] * y_b)


def _mixer_step(proj, conv0, n0, m0, c0, s0, w, *, seq_per_step):
    ns = proj.shape[0]

    def per_seq(shape):
        return pl.BlockSpec((seq_per_step,) + shape, lambda b: (b,) + (0,) * len(shape))

    row_shapes = [(ns, HIST * XBC_DIM), (ns, ML_QK), (ns, LANES)]
    mat_shapes = [(ML_HEADS, ML_DK, ML_DV), (SSM_WIDTH, SSM_STATE)]
    conv_shapes = [(CONV_W, XBC_DIM), (1, XBC_DIM)]
    in_specs = ([_const_spec((ns, PROJ_DIM), True)] + [_const_spec(s, True) for s in row_shapes]
                + [per_seq(s) for s in mat_shapes]
                + [_const_spec(s) for s in _MIXER_PARAM_SHAPES + conv_shapes])
    out_rows = [(HIST, ns, XBC_DIM)] + row_shapes[1:]
    out_specs = ([_const_spec((ns, D_MODEL))] + [_const_spec(s) for s in out_rows]
                 + [per_seq(s) for s in mat_shapes])
    out_shape = ([jax.ShapeDtypeStruct((ns, D_MODEL), F32)]
                 + [jax.ShapeDtypeStruct(s, F32) for s in out_rows]
                 + [jax.ShapeDtypeStruct((ns,) + s, F32) for s in mat_shapes])
    scratch = [
        pltpu.VMEM((ns, ML_QK), F32),
        pltpu.VMEM((ML_QK, ns), F32),
        pltpu.VMEM((ns, D_MODEL), BF16),
        pltpu.VMEM((ns, SSM_GROUPS * SSM_STATE), F32),
        pltpu.VMEM((ns, SSM_GROUPS * SSM_STATE), BF16),
        pltpu.VMEM((SSM_WIDTH, ns), F32),
        pltpu.VMEM((ns, LANES), F32),
        pltpu.VMEM((ns, D_MODEL), F32),
        pltpu.VMEM((ns, SSM_WIDTH), F32),
    ]
    return pl.pallas_call(
        _step_kernel,
        grid=(ns // seq_per_step,),
        in_specs=in_specs,
        out_specs=out_specs,
        out_shape=out_shape,
        scratch_shapes=scratch,
        compiler_params=_params(1),
        name="mixer_step",
    )(proj, conv0, n0, m0, c0, s0, *_mixer_params(w), w["ssm_conv_w"], w["ssm_conv_b"])


def _pad_lanes(vec, offset):
    pad = [(0, 0)] * (vec.ndim - 1) + [(offset, LANES - offset - vec.shape[-1])]
    return jnp.pad(vec.astype(F32), pad)


def kernel(x_prompt, x_sample, state_conv, state_mlstm_C, state_mlstm_n, state_mlstm_m, state_ssm, ffn1_norm, ffn1_w_gate, ffn1_w_up, ffn1_w_down, mix_norm, w_in, ml_i_bias, ml_f_bias, ml_head_norm, ssm_conv_w, ssm_conv_b, ssm_dt_bias, ssm_A_log, ssm_D, ssm_norm, w_out, ffn2_norm, ffn2_w_gate, ffn2_w_up, ffn2_w_down, final_norm):
    depth = w_in.shape[0]
    assert depth == 1, "the final norm is fused into the (single) layer's second FFN"
    bp, seq, d = x_prompt.shape
    bs = x_sample.shape[0]
    l = 0
    tm = 1024
    assert x_sample.shape[1] == 1 and seq % CHUNK == 0 and seq % tm == 0 and bs % SUBLANES == 0

    head_of_lane = jnp.arange(SSM_WIDTH) // SSM_HEADDIM
    w = {
        "bias_row": (_pad_lanes(ssm_dt_bias[l], SM_DT) + _pad_lanes(ml_i_bias[l], SM_IG)
                     + _pad_lanes(ml_f_bias[l], SM_FG)).reshape(1, LANES),
        "alog_row": _pad_lanes(ssm_A_log[l], SM_DT).reshape(1, LANES),
        "ml_head_norm": ml_head_norm[l], "ssm_conv_w": ssm_conv_w[l],
        "ssm_conv_b": ssm_conv_b[l].reshape(1, XBC_DIM),
        "dskip_row": jnp.repeat(ssm_D[l].astype(F32), SSM_HEADDIM).reshape(1, SSM_WIDTH),
        "ssm_norm": ssm_norm[l],
        "e2": ((jnp.arange(2 * LANES) % LANES)[:, None] == head_of_lane[None, :]).astype(BF16),
    }

    xp = x_prompt.reshape(bp * seq, d)
    xs = x_sample.reshape(bs, d)
    xp1, xs1 = _ffn(xp, xs, ffn1_norm[l], ffn1_w_gate[l], ffn1_w_up[l], ffn1_w_down[l], final_norm,
                    final_norm=False, tm=tm, tf=256)
    w_in_r = _regroup_w_in(jnp.swapaxes(w_in[l], 0, 1), tc=256)
    proj_p, proj_s, p_tail = _inproj(xp1, xs1, mix_norm[l], w_in_r, _activation_rows(), w["ssm_conv_w"],
                                     w["ssm_conv_b"], tm=tm, tn=PROJ_TILE, rows_per_seq=seq)
    p_conv = p_tail[seq // tm - 1::seq // tm, SUBLANES - HIST:, :]

    mp, p_c, p_n, p_m, p_s = _mixer_chunk(proj_p, w, n_seq=bp, n_chunks=seq // CHUNK, seq_per_step=2)
    ms, s_conv, s_n, s_m, s_c, s_s = _mixer_step(
        proj_s, state_conv[l].reshape(bs, HIST * XBC_DIM), state_mlstm_n[l].reshape(bs, ML_QK),
        _pad_lanes(state_mlstm_m[l], SM_IG), state_mlstm_C[l],
        state_ssm[l].reshape(bs, SSM_WIDTH, SSM_STATE), w, seq_per_step=2)

    xp2, xs2 = _outproj(mp, ms, w_out[l], xp1, xs1, tm=2 * tm, tn=512)
    yp, ys = _ffn(xp2, xs2, ffn2_norm[l], ffn2_w_gate[l], ffn2_w_up[l], ffn2_w_down[l], final_norm,
                  final_norm=True, tm=tm, tf=256)

    ssm_shape = (SSM_HEADS, SSM_HEADDIM, SSM_STATE)
    return (yp.reshape(bp, seq, d), ys.reshape(bs, 1, d),
            p_conv[None], p_c[None], p_n[None], p_m[:, 0, :ML_HEADS][None], p_s.reshape(bp, *ssm_shape)[None],
            jnp.swapaxes(s_conv, 0, 1)[None], s_c[None], s_n.reshape(bs, ML_HEADS, ML_DK)[None],
            s_m[:, SM_IG:SM_IG + ML_HEADS][None], s_s.reshape(bs, *ssm_shape)[None])
```

```python
import functools

import numpy as np
import jax
import jax.numpy as jnp
from jax import lax
from jax.experimental import pallas as pl
from jax.experimental.pallas import tpu as pltpu

F32 = jnp.float32
BF16 = jnp.bfloat16

EPS = 1e-6
NEG_INIT = -1e30

D_MODEL = 2048
ML_HEADS = 4
ML_DV = 512
ML_DK = 256
ML_QK = ML_HEADS * ML_DK
SSM_HEADS = 32
SSM_HEADDIM = 64
SSM_GROUPS = 2
SSM_STATE = 128
SSM_WIDTH = SSM_HEADS * SSM_HEADDIM
CONV_W = 4
HIST = CONV_W - 1
BC_DIM = 2 * SSM_GROUPS * SSM_STATE
XBC_DIM = SSM_WIDTH + BC_DIM
CHUNK = 128
IN_SIZES = (ML_QK, ML_QK, D_MODEL, ML_HEADS, ML_HEADS, D_MODEL, SSM_WIDTH, XBC_DIM, SSM_HEADS, 2 * D_MODEL)

LANES = 128
SUBLANES = 8
MXU_WIDTH = 256

SEG_XS = 0
SEG_BC = SEG_XS + SSM_WIDTH
SEG_GATES = SEG_BC + BC_DIM
SEG_OG = SEG_GATES + 2 * D_MODEL
SEG_Z = SEG_OG + D_MODEL
SEG_V = SEG_Z + SSM_WIDTH
SEG_Q = SEG_V + D_MODEL
SEG_K = SEG_Q + ML_QK
SEG_SMALL = SEG_K + ML_QK
PROJ_TILE = 1280
PROJ_DIM = -(-(SEG_SMALL + LANES) // PROJ_TILE) * PROJ_TILE
assert SEG_XS == 0 and XBC_DIM % PROJ_TILE == 0
CONV_TILES = XBC_DIM // PROJ_TILE
ACT_TILES = -(-SEG_V // PROJ_TILE)
SIG_TILES = (-(-SEG_GATES // PROJ_TILE), SEG_Z // PROJ_TILE)
SM_DT = 0
SM_IG = SM_DT + SSM_HEADS
SM_FG = SM_IG + ML_HEADS

VMEM_LIMIT = 60 * 1024 * 1024

ROW_TILE = 1024
FFN_TILE = 256
OUT_ROW_TILE = 2 * ROW_TILE
OUT_COL_TILE = 512
REGROUP_COLS = 256
SEQ_PER_STEP = 2


def _rms(x):
    return x * lax.rsqrt(jnp.mean(x * x, axis=-1, keepdims=True) + EPS)


def _sigmoid(x):
    return 0.5 * jnp.tanh(0.5 * x) + 0.5


def _silu(x):
    h = 0.5 * x
    return h * jnp.tanh(h) + h


def _softplus(x):
    return jnp.maximum(x, 0.0) + jnp.log(1.0 + jnp.exp(-jnp.abs(x)))


def _dot(a, b):
    return jnp.dot(a, b, preferred_element_type=F32)


def _dot_nt(a, b):
    return lax.dot_general(a, b, (((1,), (1,)), ((), ())), preferred_element_type=F32)


def _dot_tn(a, b):
    return lax.dot_general(a, b, (((0,), (0,)), ((), ())), preferred_element_type=F32)


def _split3(x):
    hi = x.astype(BF16)
    r1 = x - hi.astype(F32)
    mid = r1.astype(BF16)
    lo = (r1 - mid.astype(F32)).astype(BF16)
    return hi, mid, lo


def _expand_heads(x, e2):
    x_hi, x_mid, x_lo = _split3(x)
    return _dot(jnp.concatenate([x_hi, x_mid], axis=1), e2) + _dot(x_lo, e2[0:LANES, :])


def _params(dims=2):
    return pltpu.CompilerParams(dimension_semantics=("arbitrary",) * dims, vmem_limit_bytes=VMEM_LIMIT)


def _ffn_kernel(xp_ref, xs_ref, nw_ref, wg_ref, wu_ref, wd_ref, fw_ref, op_ref, os_ref, hp_ref, hs_ref,
                *, final_norm):
    i = pl.program_id(0)
    j = pl.program_id(1)
    last_j = pl.num_programs(1) - 1

    def rows(x_ref, o_ref, h_ref):
        @pl.when(j == 0)
        def _():
            h_ref[...] = (_rms(x_ref[...]) * nw_ref[...]).astype(BF16)
            o_ref[...] = jnp.zeros(o_ref.shape, F32)

        h = h_ref[...]
        g = _dot(h, wg_ref[...].astype(BF16))
        u = _dot(h, wu_ref[...].astype(BF16))
        a = (_silu(g) * u).astype(BF16)
        o_ref[...] += _dot(a, wd_ref[...].astype(BF16))

        @pl.when(j == last_j)
        def _():
            strip = min(x_ref.shape[0], LANES)

            def finish(r, carry):
                rows_ = pl.ds(pl.multiple_of(r * strip, strip), strip)
                y = x_ref[rows_, :] + 0.5 * o_ref[rows_, :]
                if final_norm:
                    y = _rms(y) * fw_ref[...]
                o_ref[rows_, :] = y
                return carry

            lax.fori_loop(0, x_ref.shape[0] // strip, finish, 0)

    rows(xp_ref, op_ref, hp_ref)

    @pl.when(i == pl.num_programs(0) - 1)
    def _():
        rows(xs_ref, os_ref, hs_ref)


def _ffn(xp, xs, norm_w, w_gate, w_up, w_down, final_w, *, final_norm, tm, tf):
    mp, d = xp.shape
    ms = xs.shape[0]
    f = w_gate.shape[1]
    return pl.pallas_call(
        functools.partial(_ffn_kernel, final_norm=final_norm),
        grid=(mp // tm, f // tf),
        in_specs=[
            pl.BlockSpec((tm, d), lambda i, j: (i, 0)),
            pl.BlockSpec((ms, d), lambda i, j: (0, 0), pipeline_mode=pl.Buffered(1)),
            pl.BlockSpec((1, d), lambda i, j: (0, 0)),
            pl.BlockSpec((d, tf), lambda i, j: (0, j)),
            pl.BlockSpec((d, tf), lambda i, j: (0, j)),
            pl.BlockSpec((tf, d), lambda i, j: (j, 0)),
            pl.BlockSpec((1, d), lambda i, j: (0, 0)),
        ],
        out_specs=[pl.BlockSpec((tm, d), lambda i, j: (i, 0)),
                   pl.BlockSpec((ms, d), lambda i, j: (0, 0))],
        out_shape=[jax.ShapeDtypeStruct((mp, d), F32), jax.ShapeDtypeStruct((ms, d), F32)],
        scratch_shapes=[pltpu.VMEM((tm, d), BF16), pltpu.VMEM((ms, d), BF16)],
        compiler_params=_params(),
        name="ffn",
    )(xp, xs, norm_w.reshape(1, d), w_gate, w_up, w_down, final_w.reshape(1, d))


def _causal_conv(cur, prev, w_ref, b_ref):
    row8 = lax.broadcasted_iota(jnp.int32, (SUBLANES, 1), 0)
    acc = cur * w_ref[HIST:HIST + 1, :] + b_ref[...]
    for back in range(1, CONV_W):
        sh = pltpu.roll(cur, back, axis=0)
        head = jnp.where(row8 < back, pltpu.roll(prev, back, axis=0), sh[0:SUBLANES])
        sh = jnp.concatenate([head, sh[SUBLANES:]], axis=0)
        acc = acc + sh * w_ref[HIST - back:HIST - back + 1, :]
    return acc


def _inproj_kernel(xp_ref, xs_ref, nw_ref, w_ref, act_ref, cw_ref, cb_ref, op_ref, os_ref, tail_ref,
                   up_ref, us_ref, hist_ref, raw_ref, *, blocks_per_seq):
    i = pl.program_id(0)
    j = pl.program_id(1)
    tm = xp_ref.shape[0]

    @pl.when(j == 0)
    def _():
        up_ref[...] = (_rms(xp_ref[...]) * nw_ref[...]).astype(BF16)

    @pl.when((i == 0) & (j == 0))
    def _():
        hist_ref[...] = jnp.zeros(hist_ref.shape, F32)

    def activate(x, cols):
        return ((act_ref[0:1, cols] + act_ref[1:2, cols] * x) * _sigmoid(x)
                + act_ref[2:3, cols] * x)

    is_conv = j < CONV_TILES
    is_sig = (j >= SIG_TILES[0]) & (j < SIG_TILES[1])
    is_act = (j >= CONV_TILES) & (j < ACT_TILES) & jnp.logical_not(is_sig)
    strips = [slice(s, s + MXU_WIDTH) for s in range(0, w_ref.shape[0], MXU_WIDTH)]

    @pl.when(is_conv)
    def _():
        jj = jnp.minimum(j, CONV_TILES - 1)
        fresh = i % blocks_per_seq == 0
        for n, cols in enumerate(strips):
            raw_ref[n % 2] = _dot_nt(up_ref[...], w_ref[cols, :])
            raw = raw_ref[n % 2]
            prev = jnp.where(fresh, 0.0, hist_ref[jj, :, cols])
            tail = raw[tm - SUBLANES:tm]
            hist_ref[jj, :, cols] = tail
            tail_ref[0, :, cols] = tail
            op_ref[:, cols] = _silu(_causal_conv(raw, prev, cw_ref.at[:, cols], cb_ref.at[:, cols]))

    @pl.when(is_sig)
    def _():
        for n, cols in enumerate(strips):
            raw_ref[n % 2] = _dot_nt(up_ref[...], w_ref[cols, :])
            op_ref[:, cols] = _sigmoid(raw_ref[n % 2])

    @pl.when(is_act)
    def _():
        for n, cols in enumerate(strips):
            raw_ref[n % 2] = _dot_nt(up_ref[...], w_ref[cols, :])
            op_ref[:, cols] = activate(raw_ref[n % 2], cols)

    @pl.when(j >= ACT_TILES)
    def _():
        op_ref[...] = _dot_nt(up_ref[...], w_ref[...])

    @pl.when(i == pl.num_programs(0) - 1)
    def _():
        @pl.when(j == 0)
        def _():
            us_ref[...] = (_rms(xs_ref[...]) * nw_ref[...]).astype(BF16)

        raw = _dot_nt(us_ref[...], w_ref[...])
        os_ref[...] = jnp.where(is_act | is_sig, activate(raw, slice(None)), raw)


def _inproj(xp, xs, norm_w, wt, act, conv_w, conv_b, *, tm, tn, rows_per_seq):
    mp, d = xp.shape
    ms = xs.shape[0]
    n = wt.shape[0]
    last_i = mp // tm - 1
    blocks_per_seq = rows_per_seq // tm

    def conv_tile(i, j):
        return (0, jnp.minimum(j, CONV_TILES - 1))

    return pl.pallas_call(
        functools.partial(_inproj_kernel, blocks_per_seq=blocks_per_seq),
        grid=(mp // tm, n // tn),
        in_specs=[
            pl.BlockSpec((tm, d), lambda i, j: (i, 0)),
            pl.BlockSpec((ms, d), lambda i, j: (0, 0), pipeline_mode=pl.Buffered(1)),
            pl.BlockSpec((1, d), lambda i, j: (0, 0)),
            pl.BlockSpec((tn, d), lambda i, j: (j, 0)),
            pl.BlockSpec((SUBLANES, tn), lambda i, j: (0, j)),
            pl.BlockSpec((CONV_W, tn), conv_tile),
            pl.BlockSpec((1, tn), conv_tile),
        ],
        out_specs=[pl.BlockSpec((tm, tn), lambda i, j: (i, j)),
                   pl.BlockSpec((ms, tn), lambda i, j: (0, jnp.where(i == last_i, j, 0))),
                   pl.BlockSpec((1, SUBLANES, tn), lambda i, j: (i, 0, jnp.minimum(j, CONV_TILES - 1)))],
        out_shape=[jax.ShapeDtypeStruct((mp, n), F32), jax.ShapeDtypeStruct((ms, n), F32),
                   jax.ShapeDtypeStruct((mp // tm, SUBLANES, XBC_DIM), F32)],
        scratch_shapes=[pltpu.VMEM((tm, d), BF16), pltpu.VMEM((ms, d), BF16),
                        pltpu.VMEM((CONV_TILES, SUBLANES, tn), F32),
                        pltpu.VMEM((2, tm, MXU_WIDTH), F32)],
        compiler_params=_params(),
        name="inproj",
    )(xp, xs, norm_w.reshape(1, d), wt, act, conv_w, conv_b)


def _outproj_kernel(yp_ref, ys_ref, w_ref, xp_ref, xs_ref, op_ref, os_ref):
    w = w_ref[...].astype(BF16)
    op_ref[...] = xp_ref[...] + _dot(yp_ref[...].astype(BF16), w)

    @pl.when(pl.program_id(0) == pl.num_programs(0) - 1)
    def _():
        os_ref[...] = xs_ref[...] + _dot(ys_ref[...].astype(BF16), w)


def _outproj(yp, ys, w, xp, xs, *, tm, tn):
    mp, d = yp.shape
    ms = ys.shape[0]
    n = w.shape[1]
    last_i = mp // tm - 1

    def sample_tile(i, j):
        return (0, jnp.where(i == last_i, j, 0))

    return pl.pallas_call(
        _outproj_kernel,
        grid=(mp // tm, n // tn),
        in_specs=[
            pl.BlockSpec((tm, d), lambda i, j: (i, 0)),
            pl.BlockSpec((ms, d), lambda i, j: (0, 0)),
            pl.BlockSpec((d, tn), lambda i, j: (0, j)),
            pl.BlockSpec((tm, tn), lambda i, j: (i, j)),
            pl.BlockSpec((ms, tn), sample_tile),
        ],
        out_specs=[pl.BlockSpec((tm, tn), lambda i, j: (i, j)),
                   pl.BlockSpec((ms, tn), sample_tile)],
        out_shape=[jax.ShapeDtypeStruct((mp, n), F32), jax.ShapeDtypeStruct((ms, n), F32)],
        compiler_params=_params(),
        name="outproj",
    )(yp, ys, w, xp, xs)


def _seg_sources():
    offs = [0]
    for s in IN_SIZES:
        offs.append(offs[-1] + s)
    q, k, v, ig, fg, og, z, xbc, dtr, gates = offs[:-1]
    assert fg == ig + ML_HEADS and SM_FG == SM_IG + ML_HEADS
    return [(SEG_GATES, gates, 2 * D_MODEL), (SEG_OG, og, D_MODEL), (SEG_Z, z, SSM_WIDTH),
            (SEG_V, v, D_MODEL), (SEG_XS, xbc, SSM_WIDTH), (SEG_Q, q, ML_QK), (SEG_K, k, ML_QK),
            (SEG_BC, xbc + SSM_WIDTH, BC_DIM), (SEG_SMALL + SM_DT, dtr, SSM_HEADS),
            (SEG_SMALL + SM_IG, ig, 2 * ML_HEADS)]


def _activation_rows():
    col = np.arange(PROJ_DIM)
    sig = (col >= SEG_GATES) & (col < SEG_Z)
    silu = (col >= SEG_Z) & (col < SEG_V)
    rows = np.zeros((SUBLANES, PROJ_DIM), np.float32)
    rows[0], rows[1], rows[2] = sig, silu, ~(sig | silu)
    return jnp.asarray(rows)


def _regroup_kernel(wt_ref, o_ref):
    cols = wt_ref.shape[1]
    small = []
    for dst, src, width in _seg_sources():
        assert src % SUBLANES == 0 and width % SUBLANES == 0
        if dst < SEG_SMALL:
            o_ref[dst:dst + width, :] = wt_ref[src:src + width, :].astype(BF16)
        else:
            assert dst - SEG_SMALL == sum(p.shape[0] for p in small)
            small.append(wt_ref[src:src + width, :])
    used = sum(p.shape[0] for p in small)
    small.append(jnp.zeros((PROJ_DIM - SEG_SMALL - used, cols), F32))
    o_ref[SEG_SMALL:PROJ_DIM, :] = jnp.concatenate(small, axis=0).astype(BF16)


def _regroup_w_in(wt, *, tc):
    n_in, d = wt.shape
    return pl.pallas_call(
        _regroup_kernel,
        grid=(d // tc,),
        in_specs=[pl.BlockSpec((n_in, tc), lambda i: (0, i))],
        out_specs=pl.BlockSpec((PROJ_DIM, tc), lambda i: (0, i)),
        out_shape=jax.ShapeDtypeStruct((PROJ_DIM, d), BF16),
        compiler_params=_params(1),
        name="regroup",
    )(wt)


def _mixer_kernel(proj_ref, bias_ref, alog_ref, dskip_ref, hnorm_ref, snorm_ref, e2_ref,
                  merged_ref, c_ref, n_ref, m_ref, s_ref):
    c = pl.program_id(1)

    @pl.when(c == 0)
    def _():
        c_ref[...] = jnp.zeros(c_ref.shape, F32)
        n_ref[...] = jnp.zeros(n_ref.shape, F32)
        m_ref[...] = jnp.full(m_ref.shape, NEG_INIT, F32)
        s_ref[...] = jnp.zeros(s_ref.shape, F32)

    for p in range(proj_ref.shape[0]):
        _mixer_sequence(proj_ref.at[p], bias_ref, alog_ref, dskip_ref, hnorm_ref, snorm_ref, e2_ref,
                        merged_ref.at[p], c_ref.at[p], n_ref.at[p], m_ref.at[p], s_ref.at[p])


def _mixer_sequence(proj_ref, bias_ref, alog_ref, dskip_ref, hnorm_ref, snorm_ref, e2_ref,
                    merged_ref, c_ref, n_ref, m_ref, s_ref):
    xs_act = proj_ref[:, SEG_XS:SEG_XS + SSM_WIDTH]
    bc_act = proj_ref[:, SEG_BC:SEG_BC + BC_DIM]

    lane = lax.broadcasted_iota(jnp.int32, (CHUNK, LANES), 1)
    row = lax.broadcasted_iota(jnp.int32, (CHUNK, LANES), 0)
    lane1 = lax.broadcasted_iota(jnp.int32, (1, LANES), 1)
    is_dt = lane < SM_IG
    is_li = (lane >= SM_IG) & (lane < SM_FG)
    is_lf = (lane >= SM_FG) & (lane < SM_FG + ML_HEADS)
    pre = proj_ref[:, SEG_SMALL:SEG_SMALL + LANES] + bias_ref[...]
    dt = jnp.where(is_dt, _softplus(pre), 0.0)
    lf = jnp.where(is_lf, -_softplus(-pre), 0.0)
    li = jnp.where(is_li, pre, 0.0)
    a_row = jnp.where(lane1 < SM_IG, -jnp.exp(alog_ref[...]), 0.0)
    causal = row >= lane
    tril = jnp.where(causal, 1.0, 0.0).astype(BF16)
    hi, mid, lo = _split3(dt * a_row + lf)
    cs = _dot(tril, hi) + _dot(tril, mid) + _dot(tril, lo)
    cs_t = cs.T
    li_t = li.T
    dt_t = dt.T
    b_last = cs[CHUNK - 1:CHUNK, :]
    ecs = jnp.exp(cs)

    q_all = (proj_ref[:, SEG_Q:SEG_Q + ML_QK] * (ML_DK ** -0.5)).astype(BF16)
    k_all = proj_ref[:, SEG_K:SEG_K + ML_QK]
    v_all = proj_ref[:, SEG_V:SEG_V + D_MODEL].astype(BF16)
    og_gate = proj_ref[:, SEG_OG:SEG_OG + D_MODEL]
    m_old = m_ref[...]
    m_new_row = m_old
    y_a_parts = []
    for h in range(ML_HEADS):
        ck = SM_FG + h
        b_col = cs[:, ck:ck + 1]
        b_row = cs_t[ck:ck + 1, :]
        li_row = li_t[SM_IG + h:SM_IG + h + 1, :]
        li_col = li[:, SM_IG + h:SM_IG + h + 1]
        m_prev = m_old[:, h:h + 1]
        dmat = jnp.where(causal, b_col - b_row + li_row, -jnp.inf)
        inter = b_col + m_prev
        m_t = jnp.maximum(inter, jnp.max(dmat, axis=1, keepdims=True))
        w_intra = jnp.exp(dmat - m_t)
        w_inter = jnp.exp(inter - m_t)
        qh = q_all[:, h * ML_DK:(h + 1) * ML_DK]
        kh = k_all[:, h * ML_DK:(h + 1) * ML_DK]
        vh = v_all[:, h * ML_DV:(h + 1) * ML_DV]
        c_old = c_ref[h]
        n_old = n_ref[h:h + 1, :]
        s = _dot_nt(qh, kh.astype(BF16)) * w_intra
        num = _dot(s.astype(BF16), vh) + w_inter * _dot(qh, c_old.astype(BF16))
        qn = jnp.sum(qh.astype(F32) * n_old, axis=1, keepdims=True)
        den = jnp.sum(s, axis=1, keepdims=True) + w_inter * qn
        hh = num / jnp.maximum(jnp.abs(den), jnp.exp(-m_t))
        m_new = m_t[CHUNK - 1:CHUNK, :]
        bl = b_col[CHUNK - 1:CHUNK, :]
        w_end = jnp.exp(bl - b_col + li_col - m_new)
        decay = jnp.exp(bl + m_prev - m_new)
        kw = kh * w_end
        c_ref[h] = decay * c_old + _dot_tn(kw.astype(BF16), vh)
        n_ref[h:h + 1, :] = decay * n_old + jnp.sum(kw, axis=0, keepdims=True)
        m_new_row = jnp.where(lane1 == h, m_new, m_new_row)
        hn = _rms(hh) * hnorm_ref[:, h * ML_DV:(h + 1) * ML_DV]
        y_a_parts.append(og_gate[:, h * ML_DV:(h + 1) * ML_DV] * hn)
    m_ref[...] = m_new_row
    y_a = jnp.concatenate(y_a_parts, axis=1)

    e2 = e2_ref[...]
    e_b = _expand_heads(jnp.where(is_dt, ecs, 0.0), e2)
    e_w = _expand_heads(jnp.exp(b_last - cs) * dt, e2)
    xs_bf = xs_act.astype(BF16)
    xw_bf = (xs_act * e_w).astype(BF16)
    half = lane < SSM_HEADDIM
    gw = SSM_WIDTH // SSM_GROUPS
    hpg = SSM_HEADS // SSM_GROUPS
    y_parts = []
    for g in range(SSM_GROUPS):
        bg = bc_act[:, g * SSM_STATE:(g + 1) * SSM_STATE].astype(BF16)
        cg = bc_act[:, (SSM_GROUPS + g) * SSM_STATE:(SSM_GROUPS + g + 1) * SSM_STATE].astype(BF16)
        cb = _dot_nt(cg, bg)
        s_old = s_ref[g * gw:(g + 1) * gw, :]
        y_inter = _dot_nt(cg, s_old.astype(BF16))
        pairs = []
        for j in range(hpg // 2):
            x_pair = xs_bf[:, g * gw + j * LANES:g * gw + (j + 1) * LANES]
            res = []
            for hh_ in range(2):
                h = g * hpg + 2 * j + hh_
                dec = jnp.exp(jnp.where(causal, cs[:, h:h + 1] - cs_t[h:h + 1, :], -jnp.inf))
                mm = (cb * dec * dt_t[h:h + 1, :]).astype(BF16)
                res.append(_dot(mm, x_pair))
            pairs.append(jnp.where(half, res[0], res[1]))
        y_intra = jnp.concatenate(pairs, axis=1)
        y_parts.append(y_intra + e_b[:, g * gw:(g + 1) * gw] * y_inter)
        upd = _dot_tn(xw_bf[:, g * gw:(g + 1) * gw], bg)
        for hh_ in range(hpg):
            h = g * hpg + hh_
            lo_r = g * gw + hh_ * SSM_HEADDIM
            s_ref[lo_r:lo_r + SSM_HEADDIM, :] = (
                ecs[CHUNK - 1:CHUNK, h:h + 1] * s_ref[lo_r:lo_r + SSM_HEADDIM, :]
                + upd[hh_ * SSM_HEADDIM:(hh_ + 1) * SSM_HEADDIM, :])
    y = jnp.concatenate(y_parts, axis=1)
    y_s = (y + dskip_ref[...] * xs_act) * proj_ref[:, SEG_Z:SEG_Z + SSM_WIDTH]
    y_b = jnp.concatenate(
        [_rms(y_s[:, g * gw:(g + 1) * gw]) for g in range(SSM_GROUPS)], axis=1) * snorm_ref[...]

    merged_ref[...] = (proj_ref[:, SEG_GATES:SEG_GATES + D_MODEL] * y_a
                       + proj_ref[:, SEG_GATES + D_MODEL:SEG_GATES + 2 * D_MODEL] * y_b
                       ).astype(merged_ref.dtype)


def _const_spec(shape, single_buffer=False):
    kwargs = dict(pipeline_mode=pl.Buffered(1)) if single_buffer else {}
    return pl.BlockSpec(shape, lambda *_: (0,) * len(shape), **kwargs)


def _mixer_params(w):
    return [w["bias_row"], w["alog_row"], w["dskip_row"], w["ml_head_norm"].reshape(1, D_MODEL),
            w["ssm_norm"].reshape(1, SSM_WIDTH), w["e2"]]


_MIXER_PARAM_SHAPES = [(1, LANES), (1, LANES), (1, SSM_WIDTH), (1, D_MODEL), (1, SSM_WIDTH),
                       (2 * LANES, SSM_WIDTH)]


def _mixer_chunk(proj, w, *, n_seq, n_chunks, seq_per_step):
    def per_seq(shape):
        return pl.BlockSpec((seq_per_step,) + shape, lambda b, c: (b,) + (0,) * len(shape))

    state_shapes = [(ML_HEADS, ML_DK, ML_DV), (ML_HEADS, ML_DK), (1, LANES), (SSM_WIDTH, SSM_STATE)]
    in_specs = [pl.BlockSpec((seq_per_step, CHUNK, PROJ_DIM), lambda b, c: (b, c, 0))] + [
        _const_spec(s) for s in _MIXER_PARAM_SHAPES]
    out_specs = [pl.BlockSpec((seq_per_step, CHUNK, D_MODEL), lambda b, c: (b, c, 0))] + [
        per_seq(s) for s in state_shapes]
    out_shape = [jax.ShapeDtypeStruct((n_seq, n_chunks * CHUNK, D_MODEL), BF16)] + [
        jax.ShapeDtypeStruct((n_seq,) + s, F32) for s in state_shapes]
    merged, *states = pl.pallas_call(
        _mixer_kernel,
        grid=(n_seq // seq_per_step, n_chunks),
        in_specs=in_specs,
        out_specs=out_specs,
        out_shape=out_shape,
        compiler_params=_params(),
        name="mixer_chunk",
    )(proj.reshape(n_seq, n_chunks * CHUNK, PROJ_DIM), *_mixer_params(w))
    return (merged.reshape(n_seq * n_chunks * CHUNK, D_MODEL), *states)


def _step_kernel(proj_ref, conv0_ref, n0_ref, m0_ref, c0_ref, s0_ref,
                 bias_ref, alog_ref, dskip_ref, hnorm_ref, snorm_ref, e2_ref, convw_ref, convb_ref,
                 merged_ref, conv_ref, n_ref, m_ref, c_ref, s_ref,
                 qf_ref, kt_ref, vb_ref, cmf_ref, bb_ref, xwt_ref, dec_ref, accq_ref, accs_ref):
    b = pl.program_id(0)
    ns = proj_ref.shape[0]
    gw = SSM_WIDTH // SSM_GROUPS
    hpg = SSM_HEADS // SSM_GROUPS
    lane = lax.broadcasted_iota(jnp.int32, (ns, LANES), 1)
    lane1 = lax.broadcasted_iota(jnp.int32, (1, LANES), 1)
    is_dt = lane < SM_IG
    is_li = (lane >= SM_IG) & (lane < SM_FG)
    is_lf = (lane >= SM_FG) & (lane < SM_FG + ML_HEADS)

    def gate_scalars():
        pre = proj_ref[:, SEG_SMALL:SEG_SMALL + LANES] + bias_ref[...]
        dt = jnp.where(is_dt, _softplus(pre), 0.0)
        lf = pltpu.roll(jnp.where(is_lf, -_softplus(-pre), 0.0), LANES - ML_HEADS, axis=1)
        li = jnp.where(is_li, pre, 0.0)
        inter = lf + m0_ref[...]
        m_t = jnp.maximum(inter, li)
        a_row = jnp.where(lane1 < SM_IG, -jnp.exp(alog_ref[...]), 0.0)
        return dict(dt=dt, m_t=m_t, w_intra=jnp.exp(li - m_t), w_inter=jnp.exp(inter - m_t),
                    e=jnp.exp(dt * a_row))

    def conv_act(lo, hi, seg, store):
        x_new = proj_ref[:, seg:seg + hi - lo]
        st = [conv0_ref[:, j * XBC_DIM + lo:j * XBC_DIM + hi] for j in range(HIST)]
        acc = st[0] * convw_ref[0:1, lo:hi]
        for j in range(1, HIST):
            acc = acc + st[j] * convw_ref[j:j + 1, lo:hi]
        acc = acc + x_new * convw_ref[HIST:HIST + 1, lo:hi]
        if store:
            for j in range(1, HIST):
                conv_ref[j - 1, :, lo:hi] = st[j]
            conv_ref[HIST - 1, :, lo:hi] = x_new
        return _silu(acc + convb_ref[:, lo:hi])

    def head_col(x, h):
        return x[:, SM_IG + h:SM_IG + h + 1]

    @pl.when(b == 0)
    def _():
        gs = gate_scalars()
        m_ref[...] = gs["m_t"]
        dec_ref[...] = jnp.where(is_dt, gs["e"], gs["w_inter"])
        qf_ref[...] = (proj_ref[:, SEG_Q:SEG_Q + ML_QK] * (ML_DK ** -0.5)).astype(BF16).astype(F32)
        vb_ref[...] = proj_ref[:, SEG_V:SEG_V + D_MODEL].astype(BF16)
        for h in range(ML_HEADS):
            sl = slice(h * ML_DK, (h + 1) * ML_DK)
            kw = proj_ref[:, SEG_K + h * ML_DK:SEG_K + (h + 1) * ML_DK] * head_col(gs["w_intra"], h)
            n_ref[:, sl] = head_col(gs["w_inter"], h) * n0_ref[:, sl] + kw
            kt_ref[sl, :] = kw.T
        xs_act = conv_act(0, SSM_WIDTH, SEG_XS, True)
        bc_act = conv_act(SSM_WIDTH, XBC_DIM, SEG_BC, True)
        bb_ref[...] = bc_act[:, 0:SSM_GROUPS * SSM_STATE].astype(BF16)
        cmf_ref[...] = bc_act[:, SSM_GROUPS * SSM_STATE:BC_DIM]
        xw = xs_act * _expand_heads(gs["dt"], e2_ref[...])
        for g in range(SSM_GROUPS):
            xwt_ref[g * gw:(g + 1) * gw, :] = xw[:, g * gw:(g + 1) * gw].T
        accq_ref[...] = jnp.zeros(accq_ref.shape, F32)
        accs_ref[...] = jnp.zeros(accs_ref.shape, F32)

    for sb in range(c0_ref.shape[0]):
        seq = b * c0_ref.shape[0] + sb
        base = pl.multiple_of((seq // SUBLANES) * SUBLANES, SUBLANES)
        is_row = lax.broadcasted_iota(jnp.int32, (SUBLANES, 1), 0) == seq % SUBLANES
        is_seq = lax.broadcasted_iota(jnp.int32, (1, ns), 1) == seq
        q8 = jnp.where(is_row, qf_ref[pl.ds(base, SUBLANES), :], 0.0).astype(BF16)
        c8 = jnp.where(is_row, cmf_ref[pl.ds(base, SUBLANES), :], 0.0).astype(BF16)
        drow = dec_ref[pl.ds(seq, 1), :]
        for h in range(ML_HEADS):
            c_old = c0_ref[sb, h]
            accq_ref[pl.ds(base, SUBLANES), h * ML_DV:(h + 1) * ML_DV] += _dot(
                q8[:, h * ML_DK:(h + 1) * ML_DK], c_old.astype(BF16))
            kt = jnp.where(is_seq, kt_ref[h * ML_DK:(h + 1) * ML_DK, :], 0.0).astype(BF16)
            c_ref[sb, h] = (drow[:, SM_IG + h:SM_IG + h + 1] * c_old
                            + _dot(kt, vb_ref[:, h * ML_DV:(h + 1) * ML_DV]))
        for g in range(SSM_GROUPS):
            s_old = s0_ref[sb, g * gw:(g + 1) * gw, :]
            accs_ref[pl.ds(base, SUBLANES), g * gw:(g + 1) * gw] += _dot_nt(
                c8[:, g * SSM_STATE:(g + 1) * SSM_STATE], s_old.astype(BF16))
            xwt = jnp.where(is_seq, xwt_ref[g * gw:(g + 1) * gw, :], 0.0).astype(BF16)
            upd = _dot(xwt, bb_ref[:, g * SSM_STATE:(g + 1) * SSM_STATE])
            for hh_ in range(hpg):
                h = g * hpg + hh_
                rs = slice(hh_ * SSM_HEADDIM, (hh_ + 1) * SSM_HEADDIM)
                s_ref[sb, g * gw + hh_ * SSM_HEADDIM:g * gw + (hh_ + 1) * SSM_HEADDIM, :] = (
                    drow[:, h:h + 1] * s_old[rs, :] + upd[rs, :])

    @pl.when(b == pl.num_programs(0) - 1)
    def _():
        gs = gate_scalars()
        q = qf_ref[...]
        y_a_parts = []
        for h in range(ML_HEADS):
            qh = q[:, h * ML_DK:(h + 1) * ML_DK]
            kh = proj_ref[:, SEG_K + h * ML_DK:SEG_K + (h + 1) * ML_DK].astype(BF16).astype(F32)
            vh = vb_ref[:, h * ML_DV:(h + 1) * ML_DV].astype(F32)
            w_inter = head_col(gs["w_inter"], h)
            s = jnp.sum(qh * kh, axis=1, keepdims=True) * head_col(gs["w_intra"], h)
            qn = jnp.sum(qh * n0_ref[:, h * ML_DK:(h + 1) * ML_DK], axis=1, keepdims=True)
            num = s.astype(BF16).astype(F32) * vh + w_inter * accq_ref[:, h * ML_DV:(h + 1) * ML_DV]
            den = s + w_inter * qn
            hh = num / jnp.maximum(jnp.abs(den), jnp.exp(-head_col(gs["m_t"], h)))
            hn = _rms(hh) * hnorm_ref[:, h * ML_DV:(h + 1) * ML_DV]
            y_a_parts.append(proj_ref[:, SEG_OG + h * ML_DV:SEG_OG + (h + 1) * ML_DV] * hn)
        y_a = jnp.concatenate(y_a_parts, axis=1)

        xs_act = conv_act(0, SSM_WIDTH, SEG_XS, False)
        bb = bb_ref[...].astype(F32)
        cm = cmf_ref[...].astype(BF16).astype(F32)
        e2 = e2_ref[...]
        x_dt = xs_act * _expand_heads(gs["dt"], e2)
        e_e = _expand_heads(jnp.where(is_dt, gs["e"], 0.0), e2)
        y_parts = []
        for g in range(SSM_GROUPS):
            gl = slice(g * SSM_STATE, (g + 1) * SSM_STATE)
            cb = jnp.sum(cm[:, gl] * bb[:, gl], axis=1, keepdims=True)
            cl = slice(g * gw, (g + 1) * gw)
            y_parts.append(cb * x_dt[:, cl] + e_e[:, cl] * accs_ref[:, cl])
        y = jnp.concatenate(y_parts, axis=1)
        y_s = (y + dskip_ref[...] * xs_act) * proj_ref[:, SEG_Z:SEG_Z + SSM_WIDTH]
        y_b = jnp.concatenate(
            [_rms(y_s[:, g * gw:(g + 1) * gw]) for g in range(SSM_GROUPS)], axis=1) * snorm_ref[...]
        merged_ref[...] = (proj_ref[:, SEG_GATES:SEG_GATES + D_MODEL] * y_a
                           + proj_ref[:, SEG_GATES + D_MODEL:SEG_GATES + 2 * D_MODEL] * y_b)


def _mixer_step(proj, conv0, n0, m0, c0, s0, w, *, seq_per_step):
    ns = proj.shape[0]

    def per_seq(shape):
        return pl.BlockSpec((seq_per_step,) + shape, lambda b: (b,) + (0,) * len(shape))

    row_shapes = [(ns, HIST * XBC_DIM), (ns, ML_QK), (ns, LANES)]
    mat_shapes = [(ML_HEADS, ML_DK, ML_DV), (SSM_WIDTH, SSM_STATE)]
    conv_shapes = [(CONV_W, XBC_DIM), (1, XBC_DIM)]
    in_specs = ([_const_spec((ns, PROJ_DIM), True)] + [_const_spec(s, True) for s in row_shapes]
                + [per_seq(s) for s in mat_shapes]
                + [_const_spec(s) for s in _MIXER_PARAM_SHAPES + conv_shapes])
    out_rows = [(HIST, ns, XBC_DIM)] + row_shapes[1:]
    out_specs = ([_const_spec((ns, D_MODEL))] + [_const_spec(s) for s in out_rows]
                 + [per_seq(s) for s in mat_shapes])
    out_shape = ([jax.ShapeDtypeStruct((ns, D_MODEL), F32)]
                 + [jax.ShapeDtypeStruct(s, F32) for s in out_rows]
                 + [jax.ShapeDtypeStruct((ns,) + s, F32) for s in mat_shapes])
    scratch = [
        pltpu.VMEM((ns, ML_QK), F32),
        pltpu.VMEM((ML_QK, ns), F32),
        pltpu.VMEM((ns, D_MODEL), BF16),
        pltpu.VMEM((ns, SSM_GROUPS * SSM_STATE), F32),
        pltpu.VMEM((ns, SSM_GROUPS * SSM_STATE), BF16),
        pltpu.VMEM((SSM_WIDTH, ns), F32),
        pltpu.VMEM((ns, LANES), F32),
        pltpu.VMEM((ns, D_MODEL), F32),
        pltpu.VMEM((ns, SSM_WIDTH), F32),
    ]
    return pl.pallas_call(
        _step_kernel,
        grid=(ns // seq_per_step,),
        in_specs=in_specs,
        out_specs=out_specs,
        out_shape=out_shape,
        scratch_shapes=scratch,
        compiler_params=_params(1),
        name="mixer_step",
    )(proj, conv0, n0, m0, c0, s0, *_mixer_params(w), w["ssm_conv_w"], w["ssm_conv_b"])


def _pad_lanes(vec, offset):
    pad = [(0, 0)] * (vec.ndim - 1) + [(offset, LANES - offset - vec.shape[-1])]
    return jnp.pad(vec.astype(F32), pad)


def kernel(x_prompt, x_sample, state_conv, state_mlstm_C, state_mlstm_n, state_mlstm_m, state_ssm, ffn1_norm, ffn1_w_gate, ffn1_w_up, ffn1_w_down, mix_norm, w_in, ml_i_bias, ml_f_bias, ml_head_norm, ssm_conv_w, ssm_conv_b, ssm_dt_bias, ssm_A_log, ssm_D, ssm_norm, w_out, ffn2_norm, ffn2_w_gate, ffn2_w_up, ffn2_w_down, final_norm):
    depth = w_in.shape[0]
    assert depth == 1, "the final norm is fused into the (single) layer's second FFN"
    bp, seq, d = x_prompt.shape
    bs = x_sample.shape[0]
    l = 0
    tm = ROW_TILE
    assert x_sample.shape[1] == 1 and seq % CHUNK == 0 and seq % tm == 0
    assert bp % SEQ_PER_STEP == 0 and bs % SUBLANES == 0 and SUBLANES % SEQ_PER_STEP == 0

    head_of_lane = jnp.arange(SSM_WIDTH) // SSM_HEADDIM
    w = {
        "bias_row": (_pad_lanes(ssm_dt_bias[l], SM_DT) + _pad_lanes(ml_i_bias[l], SM_IG)
                     + _pad_lanes(ml_f_bias[l], SM_FG)).reshape(1, LANES),
        "alog_row": _pad_lanes(ssm_A_log[l], SM_DT).reshape(1, LANES),
        "ml_head_norm": ml_head_norm[l], "ssm_conv_w": ssm_conv_w[l],
        "ssm_conv_b": ssm_conv_b[l].reshape(1, XBC_DIM),
        "dskip_row": jnp.repeat(ssm_D[l].astype(F32), SSM_HEADDIM).reshape(1, SSM_WIDTH),
        "ssm_norm": ssm_norm[l],
        "e2": ((jnp.arange(2 * LANES) % LANES)[:, None] == head_of_lane[None, :]).astype(BF16),
    }

    xp = x_prompt.reshape(bp * seq, d)
    xs = x_sample.reshape(bs, d)
    xp1, xs1 = _ffn(xp, xs, ffn1_norm[l], ffn1_w_gate[l], ffn1_w_up[l], ffn1_w_down[l], final_norm,
                    final_norm=False, tm=tm, tf=FFN_TILE)
    w_in_r = _regroup_w_in(jnp.swapaxes(w_in[l], 0, 1), tc=REGROUP_COLS)
    proj_p, proj_s, p_tail = _inproj(xp1, xs1, mix_norm[l], w_in_r, _activation_rows(), w["ssm_conv_w"],
                                     w["ssm_conv_b"], tm=tm, tn=PROJ_TILE, rows_per_seq=seq)
    p_conv = p_tail[seq // tm - 1::seq // tm, SUBLANES - HIST:, :]

    mp, p_c, p_n, p_m, p_s = _mixer_chunk(proj_p, w, n_seq=bp, n_chunks=seq // CHUNK,
                                          seq_per_step=SEQ_PER_STEP)
    ms, s_conv, s_n, s_m, s_c, s_s = _mixer_step(
        proj_s, state_conv[l].reshape(bs, HIST * XBC_DIM), state_mlstm_n[l].reshape(bs, ML_QK),
        _pad_lanes(state_mlstm_m[l], SM_IG), state_mlstm_C[l],
        state_ssm[l].reshape(bs, SSM_WIDTH, SSM_STATE), w, seq_per_step=SEQ_PER_STEP)

    xp2, xs2 = _outproj(mp, ms, w_out[l], xp1, xs1, tm=OUT_ROW_TILE, tn=OUT_COL_TILE)
    yp, ys = _ffn(xp2, xs2, ffn2_norm[l], ffn2_w_gate[l], ffn2_w_up[l], ffn2_w_down[l], final_norm,
                  final_norm=True, tm=tm, tf=FFN_TILE)

    ssm_shape = (SSM_HEADS, SSM_HEADDIM, SSM_STATE)
    return (yp.reshape(bp, seq, d), ys.reshape(bs, 1, d),
            p_conv[None], p_c[None], p_n[None], p_m[:, 0, :ML_HEADS][None], p_s.reshape(bp, *ssm_shape)[None],
            jnp.swapaxes(s_conv, 0, 1)[None], s_c[None], s_n.reshape(bs, ML_HEADS, ML_DK)[None],
            s_m[:, SM_IG:SM_IG + ML_HEADS][None], s_s.reshape(bs, *ssm_shape)[None])
```

```python
import functools

import numpy as np
import jax
import jax.numpy as jnp
from jax import lax
from jax.experimental import pallas as pl
from jax.experimental.pallas import tpu as pltpu

F32 = jnp.float32
BF16 = jnp.bfloat16

EPS = 1e-6
NEG_INIT = -1e30

D_MODEL = 2048
ML_HEADS = 4
ML_DV = 512
ML_DK = 256
ML_QK = ML_HEADS * ML_DK
SSM_HEADS = 32
SSM_HEADDIM = 64
SSM_GROUPS = 2
SSM_STATE = 128
SSM_WIDTH = SSM_HEADS * SSM_HEADDIM
CONV_W = 4
HIST = CONV_W - 1
BC_DIM = 2 * SSM_GROUPS * SSM_STATE
XBC_DIM = SSM_WIDTH + BC_DIM
CHUNK = 128
IN_SIZES = (ML_QK, ML_QK, D_MODEL, ML_HEADS, ML_HEADS, D_MODEL, SSM_WIDTH, XBC_DIM, SSM_HEADS, 2 * D_MODEL)

LANES = 128
SUBLANES = 8
MXU_WIDTH = 256

SEG_XS = 0
SEG_BC = SEG_XS + SSM_WIDTH
SEG_GATES = SEG_BC + BC_DIM
SEG_OG = SEG_GATES + 2 * D_MODEL
SEG_Z = SEG_OG + D_MODEL
SEG_V = SEG_Z + SSM_WIDTH
SEG_Q = SEG_V + D_MODEL
SEG_K = SEG_Q + ML_QK
SEG_SMALL = SEG_K + ML_QK
PROJ_TILE = 1280
PROJ_DIM = -(-(SEG_SMALL + LANES) // PROJ_TILE) * PROJ_TILE
assert SEG_XS == 0 and XBC_DIM % PROJ_TILE == 0
CONV_TILES = XBC_DIM // PROJ_TILE
ACT_TILES = -(-SEG_V // PROJ_TILE)
SM_DT = 0
SM_IG = SM_DT + SSM_HEADS
SM_FG = SM_IG + ML_HEADS

VMEM_LIMIT = 60 * 1024 * 1024

ROW_TILE = 1024
FFN_TILE = 256
OUT_ROW_TILE = 2 * ROW_TILE
OUT_COL_TILE = 512
REGROUP_COLS = 256
SEQ_PER_STEP = 2


def _rms(x):
    return x * lax.rsqrt(jnp.mean(x * x, axis=-1, keepdims=True) + EPS)


def _sigmoid(x):
    return 0.5 * jnp.tanh(0.5 * x) + 0.5


def _silu(x):
    h = 0.5 * x
    return h * jnp.tanh(h) + h


def _softplus(x):
    return jnp.maximum(x, 0.0) + jnp.log(1.0 + jnp.exp(-jnp.abs(x)))


def _dot(a, b):
    return jnp.dot(a, b, preferred_element_type=F32)


def _dot_nt(a, b):
    return lax.dot_general(a, b, (((1,), (1,)), ((), ())), preferred_element_type=F32)


def _dot_tn(a, b):
    return lax.dot_general(a, b, (((0,), (0,)), ((), ())), preferred_element_type=F32)


def _split3(x):
    hi = x.astype(BF16)
    r1 = x - hi.astype(F32)
    mid = r1.astype(BF16)
    lo = (r1 - mid.astype(F32)).astype(BF16)
    return hi, mid, lo


def _expand_heads(x, e2):
    x_hi, x_mid, x_lo = _split3(x)
    return _dot(jnp.concatenate([x_hi, x_mid], axis=1), e2) + _dot(x_lo, e2[0:LANES, :])


def _params(dims=2):
    return pltpu.CompilerParams(dimension_semantics=("arbitrary",) * dims, vmem_limit_bytes=VMEM_LIMIT)


def _ffn_kernel(xp_ref, xs_ref, nw_ref, wg_ref, wu_ref, wd_ref, fw_ref, op_ref, os_ref, hp_ref, hs_ref,
                *, final_norm):
    i = pl.program_id(0)
    j = pl.program_id(1)
    last_j = pl.num_programs(1) - 1

    def rows(x_ref, o_ref, h_ref):
        @pl.when(j == 0)
        def _():
            h_ref[...] = (_rms(x_ref[...]) * nw_ref[...]).astype(BF16)
            o_ref[...] = jnp.zeros(o_ref.shape, F32)

        h = h_ref[...]
        g = _dot(h, wg_ref[...].astype(BF16))
        u = _dot(h, wu_ref[...].astype(BF16))
        a = (_silu(g) * u).astype(BF16)
        o_ref[...] += _dot(a, wd_ref[...].astype(BF16))

        @pl.when(j == last_j)
        def _():
            strip = min(x_ref.shape[0], LANES)

            def finish(r, carry):
                rows_ = pl.ds(pl.multiple_of(r * strip, strip), strip)
                y = x_ref[rows_, :] + 0.5 * o_ref[rows_, :]
                if final_norm:
                    y = _rms(y) * fw_ref[...]
                o_ref[rows_, :] = y
                return carry

            lax.fori_loop(0, x_ref.shape[0] // strip, finish, 0)

    rows(xp_ref, op_ref, hp_ref)

    @pl.when(i == pl.num_programs(0) - 1)
    def _():
        rows(xs_ref, os_ref, hs_ref)


def _ffn(xp, xs, norm_w, w_gate, w_up, w_down, final_w, *, final_norm, tm, tf):
    mp, d = xp.shape
    ms = xs.shape[0]
    f = w_gate.shape[1]
    return pl.pallas_call(
        functools.partial(_ffn_kernel, final_norm=final_norm),
        grid=(mp // tm, f // tf),
        in_specs=[
            pl.BlockSpec((tm, d), lambda i, j: (i, 0)),
            pl.BlockSpec((ms, d), lambda i, j: (0, 0), pipeline_mode=pl.Buffered(1)),
            pl.BlockSpec((1, d), lambda i, j: (0, 0)),
            pl.BlockSpec((d, tf), lambda i, j: (0, j)),
            pl.BlockSpec((d, tf), lambda i, j: (0, j)),
            pl.BlockSpec((tf, d), lambda i, j: (j, 0)),
            pl.BlockSpec((1, d), lambda i, j: (0, 0)),
        ],
        out_specs=[pl.BlockSpec((tm, d), lambda i, j: (i, 0)),
                   pl.BlockSpec((ms, d), lambda i, j: (0, 0))],
        out_shape=[jax.ShapeDtypeStruct((mp, d), F32), jax.ShapeDtypeStruct((ms, d), F32)],
        scratch_shapes=[pltpu.VMEM((tm, d), BF16), pltpu.VMEM((ms, d), BF16)],
        compiler_params=_params(),
        name="ffn",
    )(xp, xs, norm_w.reshape(1, d), w_gate, w_up, w_down, final_w.reshape(1, d))


def _causal_conv(cur, prev, w_ref, b_ref):
    row8 = lax.broadcasted_iota(jnp.int32, (SUBLANES, 1), 0)
    acc = cur * w_ref[HIST:HIST + 1, :] + b_ref[...]
    for back in range(1, CONV_W):
        sh = pltpu.roll(cur, back, axis=0)
        head = jnp.where(row8 < back, pltpu.roll(prev, back, axis=0), sh[0:SUBLANES])
        sh = jnp.concatenate([head, sh[SUBLANES:]], axis=0)
        acc = acc + sh * w_ref[HIST - back:HIST - back + 1, :]
    return acc


def _inproj_kernel(xp_ref, xs_ref, nw_ref, w_ref, act_ref, cw_ref, cb_ref, op_ref, os_ref, tail_ref,
                   up_ref, us_ref, hist_ref, raw_ref, *, blocks_per_seq):
    i = pl.program_id(0)
    j = pl.program_id(1)
    tm = xp_ref.shape[0]

    @pl.when(j == 0)
    def _():
        up_ref[...] = (_rms(xp_ref[...]) * nw_ref[...]).astype(BF16)

    @pl.when((i == 0) & (j == 0))
    def _():
        hist_ref[...] = jnp.zeros(hist_ref.shape, F32)

    def activate(x, cols):
        return ((act_ref[0:1, cols] + act_ref[1:2, cols] * x) * _sigmoid(x)
                + act_ref[2:3, cols] * x)

    is_conv = j < CONV_TILES
    is_act = (j >= CONV_TILES) & (j < ACT_TILES)
    strips = [slice(s, s + MXU_WIDTH) for s in range(0, w_ref.shape[0], MXU_WIDTH)]

    @pl.when(is_conv)
    def _():
        jj = jnp.minimum(j, CONV_TILES - 1)
        fresh = i % blocks_per_seq == 0
        for n, cols in enumerate(strips):
            raw_ref[n % 2] = _dot_nt(up_ref[...], w_ref[cols, :])
            raw = raw_ref[n % 2]
            prev = jnp.where(fresh, 0.0, hist_ref[jj, :, cols])
            tail = raw[tm - SUBLANES:tm]
            hist_ref[jj, :, cols] = tail
            tail_ref[0, :, cols] = tail
            op_ref[:, cols] = _silu(_causal_conv(raw, prev, cw_ref.at[:, cols], cb_ref.at[:, cols]))

    @pl.when(is_act)
    def _():
        for n, cols in enumerate(strips):
            raw_ref[n % 2] = _dot_nt(up_ref[...], w_ref[cols, :])
            op_ref[:, cols] = activate(raw_ref[n % 2], cols)

    @pl.when(j >= ACT_TILES)
    def _():
        op_ref[...] = _dot_nt(up_ref[...], w_ref[...])

    @pl.when(i == pl.num_programs(0) - 1)
    def _():
        @pl.when(j == 0)
        def _():
            us_ref[...] = (_rms(xs_ref[...]) * nw_ref[...]).astype(BF16)

        raw = _dot_nt(us_ref[...], w_ref[...])
        os_ref[...] = jnp.where(is_act, activate(raw, slice(None)), raw)


def _inproj(xp, xs, norm_w, wt, act, conv_w, conv_b, *, tm, tn, rows_per_seq):
    mp, d = xp.shape
    ms = xs.shape[0]
    n = wt.shape[0]
    last_i = mp // tm - 1
    blocks_per_seq = rows_per_seq // tm

    def conv_tile(i, j):
        return (0, jnp.minimum(j, CONV_TILES - 1))

    return pl.pallas_call(
        functools.partial(_inproj_kernel, blocks_per_seq=blocks_per_seq),
        grid=(mp // tm, n // tn),
        in_specs=[
            pl.BlockSpec((tm, d), lambda i, j: (i, 0)),
            pl.BlockSpec((ms, d), lambda i, j: (0, 0), pipeline_mode=pl.Buffered(1)),
            pl.BlockSpec((1, d), lambda i, j: (0, 0)),
            pl.BlockSpec((tn, d), lambda i, j: (j, 0)),
            pl.BlockSpec((SUBLANES, tn), lambda i, j: (0, j)),
            pl.BlockSpec((CONV_W, tn), conv_tile),
            pl.BlockSpec((1, tn), conv_tile),
        ],
        out_specs=[pl.BlockSpec((tm, tn), lambda i, j: (i, j)),
                   pl.BlockSpec((ms, tn), lambda i, j: (0, jnp.where(i == last_i, j, 0))),
                   pl.BlockSpec((1, SUBLANES, tn), lambda i, j: (i, 0, jnp.minimum(j, CONV_TILES - 1)))],
        out_shape=[jax.ShapeDtypeStruct((mp, n), F32), jax.ShapeDtypeStruct((ms, n), F32),
                   jax.ShapeDtypeStruct((mp // tm, SUBLANES, XBC_DIM), F32)],
        scratch_shapes=[pltpu.VMEM((tm, d), BF16), pltpu.VMEM((ms, d), BF16),
                        pltpu.VMEM((CONV_TILES, SUBLANES, tn), F32),
                        pltpu.VMEM((2, tm, MXU_WIDTH), F32)],
        compiler_params=_params(),
        name="inproj",
    )(xp, xs, norm_w.reshape(1, d), wt, act, conv_w, conv_b)


def _outproj_kernel(yp_ref, ys_ref, w_ref, xp_ref, xs_ref, op_ref, os_ref):
    w = w_ref[...].astype(BF16)
    op_ref[...] = xp_ref[...] + _dot(yp_ref[...].astype(BF16), w)

    @pl.when(pl.program_id(0) == pl.num_programs(0) - 1)
    def _():
        os_ref[...] = xs_ref[...] + _dot(ys_ref[...].astype(BF16), w)


def _outproj(yp, ys, w, xp, xs, *, tm, tn):
    mp, d = yp.shape
    ms = ys.shape[0]
    n = w.shape[1]
    last_i = mp // tm - 1

    def sample_tile(i, j):
        return (0, jnp.where(i == last_i, j, 0))

    return pl.pallas_call(
        _outproj_kernel,
        grid=(mp // tm, n // tn),
        in_specs=[
            pl.BlockSpec((tm, d), lambda i, j: (i, 0)),
            pl.BlockSpec((ms, d), lambda i, j: (0, 0)),
            pl.BlockSpec((d, tn), lambda i, j: (0, j)),
            pl.BlockSpec((tm, tn), lambda i, j: (i, j)),
            pl.BlockSpec((ms, tn), sample_tile),
        ],
        out_specs=[pl.BlockSpec((tm, tn), lambda i, j: (i, j)),
                   pl.BlockSpec((ms, tn), sample_tile)],
        out_shape=[jax.ShapeDtypeStruct((mp, n), F32), jax.ShapeDtypeStruct((ms, n), F32)],
        compiler_params=_params(),
        name="outproj",
    )(yp, ys, w, xp, xs)


def _seg_sources():
    offs = [0]
    for s in IN_SIZES:
        offs.append(offs[-1] + s)
    q, k, v, ig, fg, og, z, xbc, dtr, gates = offs[:-1]
    assert fg == ig + ML_HEADS and SM_FG == SM_IG + ML_HEADS
    return [(SEG_GATES, gates, 2 * D_MODEL), (SEG_OG, og, D_MODEL), (SEG_Z, z, SSM_WIDTH),
            (SEG_V, v, D_MODEL), (SEG_XS, xbc, SSM_WIDTH), (SEG_Q, q, ML_QK), (SEG_K, k, ML_QK),
            (SEG_BC, xbc + SSM_WIDTH, BC_DIM), (SEG_SMALL + SM_DT, dtr, SSM_HEADS),
            (SEG_SMALL + SM_IG, ig, 2 * ML_HEADS)]


def _activation_rows():
    col = np.arange(PROJ_DIM)
    sig = (col >= SEG_GATES) & (col < SEG_Z)
    silu = (col >= SEG_Z) & (col < SEG_V)
    rows = np.zeros((SUBLANES, PROJ_DIM), np.float32)
    rows[0], rows[1], rows[2] = sig, silu, ~(sig | silu)
    return jnp.asarray(rows)


def _regroup_kernel(wt_ref, o_ref):
    cols = wt_ref.shape[1]
    small = []
    for dst, src, width in _seg_sources():
        assert src % SUBLANES == 0 and width % SUBLANES == 0
        if dst < SEG_SMALL:
            o_ref[dst:dst + width, :] = wt_ref[src:src + width, :].astype(BF16)
        else:
            assert dst - SEG_SMALL == sum(p.shape[0] for p in small)
            small.append(wt_ref[src:src + width, :])
    used = sum(p.shape[0] for p in small)
    small.append(jnp.zeros((PROJ_DIM - SEG_SMALL - used, cols), F32))
    o_ref[SEG_SMALL:PROJ_DIM, :] = jnp.concatenate(small, axis=0).astype(BF16)


def _regroup_w_in(wt, *, tc):
    n_in, d = wt.shape
    return pl.pallas_call(
        _regroup_kernel,
        grid=(d // tc,),
        in_specs=[pl.BlockSpec((n_in, tc), lambda i: (0, i))],
        out_specs=pl.BlockSpec((PROJ_DIM, tc), lambda i: (0, i)),
        out_shape=jax.ShapeDtypeStruct((PROJ_DIM, d), BF16),
        compiler_params=_params(1),
        name="regroup",
    )(wt)


def _mixer_kernel(proj_ref, bias_ref, alog_ref, dskip_ref, hnorm_ref, snorm_ref, e2_ref,
                  merged_ref, c_ref, n_ref, m_ref, s_ref):
    c = pl.program_id(1)

    @pl.when(c == 0)
    def _():
        c_ref[...] = jnp.zeros(c_ref.shape, F32)
        n_ref[...] = jnp.zeros(n_ref.shape, F32)
        m_ref[...] = jnp.full(m_ref.shape, NEG_INIT, F32)
        s_ref[...] = jnp.zeros(s_ref.shape, F32)

    for p in range(proj_ref.shape[0]):
        _mixer_sequence(proj_ref.at[p], bias_ref, alog_ref, dskip_ref, hnorm_ref, snorm_ref, e2_ref,
                        merged_ref.at[p], c_ref.at[p], n_ref.at[p], m_ref.at[p], s_ref.at[p])


def _mixer_sequence(proj_ref, bias_ref, alog_ref, dskip_ref, hnorm_ref, snorm_ref, e2_ref,
                    merged_ref, c_ref, n_ref, m_ref, s_ref):
    xs_act = proj_ref[:, SEG_XS:SEG_XS + SSM_WIDTH]
    bc_act = proj_ref[:, SEG_BC:SEG_BC + BC_DIM]

    lane = lax.broadcasted_iota(jnp.int32, (CHUNK, LANES), 1)
    row = lax.broadcasted_iota(jnp.int32, (CHUNK, LANES), 0)
    lane1 = lax.broadcasted_iota(jnp.int32, (1, LANES), 1)
    is_dt = lane < SM_IG
    is_li = (lane >= SM_IG) & (lane < SM_FG)
    is_lf = (lane >= SM_FG) & (lane < SM_FG + ML_HEADS)
    pre = proj_ref[:, SEG_SMALL:SEG_SMALL + LANES] + bias_ref[...]
    dt = jnp.where(is_dt, _softplus(pre), 0.0)
    lf = jnp.where(is_lf, -_softplus(-pre), 0.0)
    li = jnp.where(is_li, pre, 0.0)
    a_row = jnp.where(lane1 < SM_IG, -jnp.exp(alog_ref[...]), 0.0)
    causal = row >= lane
    tril = jnp.where(causal, 1.0, 0.0).astype(BF16)
    hi, mid, lo = _split3(dt * a_row + lf)
    cs = _dot(tril, hi) + _dot(tril, mid) + _dot(tril, lo)
    cs_t = cs.T
    li_t = li.T
    dt_t = dt.T
    b_last = cs[CHUNK - 1:CHUNK, :]
    ecs = jnp.exp(cs)

    q_all = (proj_ref[:, SEG_Q:SEG_Q + ML_QK] * (ML_DK ** -0.5)).astype(BF16)
    k_all = proj_ref[:, SEG_K:SEG_K + ML_QK]
    v_all = proj_ref[:, SEG_V:SEG_V + D_MODEL].astype(BF16)
    og_gate = proj_ref[:, SEG_OG:SEG_OG + D_MODEL]
    m_old = m_ref[...]
    m_new_row = m_old
    y_a_parts = []
    for h in range(ML_HEADS):
        ck = SM_FG + h
        b_col = cs[:, ck:ck + 1]
        b_row = cs_t[ck:ck + 1, :]
        li_row = li_t[SM_IG + h:SM_IG + h + 1, :]
        li_col = li[:, SM_IG + h:SM_IG + h + 1]
        m_prev = m_old[:, h:h + 1]
        dmat = jnp.where(causal, b_col - b_row + li_row, -jnp.inf)
        inter = b_col + m_prev
        m_t = jnp.maximum(inter, jnp.max(dmat, axis=1, keepdims=True))
        w_intra = jnp.exp(dmat - m_t)
        w_inter = jnp.exp(inter - m_t)
        qh = q_all[:, h * ML_DK:(h + 1) * ML_DK]
        kh = k_all[:, h * ML_DK:(h + 1) * ML_DK]
        vh = v_all[:, h * ML_DV:(h + 1) * ML_DV]
        c_old = c_ref[h]
        n_old = n_ref[h:h + 1, :]
        s = _dot_nt(qh, kh.astype(BF16)) * w_intra
        num = _dot(s.astype(BF16), vh) + w_inter * _dot(qh, c_old.astype(BF16))
        qn = jnp.sum(qh.astype(F32) * n_old, axis=1, keepdims=True)
        den = jnp.sum(s, axis=1, keepdims=True) + w_inter * qn
        hh = num / jnp.maximum(jnp.abs(den), jnp.exp(-m_t))
        m_new = m_t[CHUNK - 1:CHUNK, :]
        bl = b_col[CHUNK - 1:CHUNK, :]
        w_end = jnp.exp(bl - b_col + li_col - m_new)
        decay = jnp.exp(bl + m_prev - m_new)
        kw = kh * w_end
        c_ref[h] = decay * c_old + _dot_tn(kw.astype(BF16), vh)
        n_ref[h:h + 1, :] = decay * n_old + jnp.sum(kw, axis=0, keepdims=True)
        m_new_row = jnp.where(lane1 == h, m_new, m_new_row)
        hn = _rms(hh) * hnorm_ref[:, h * ML_DV:(h + 1) * ML_DV]
        y_a_parts.append(og_gate[:, h * ML_DV:(h + 1) * ML_DV] * hn)
    m_ref[...] = m_new_row
    y_a = jnp.concatenate(y_a_parts, axis=1)

    e2 = e2_ref[...]
    e_b = _expand_heads(jnp.where(is_dt, ecs, 0.0), e2)
    e_w = _expand_heads(jnp.exp(b_last - cs) * dt, e2)
    xs_bf = xs_act.astype(BF16)
    xw_bf = (xs_act * e_w).astype(BF16)
    half = lane < SSM_HEADDIM
    gw = SSM_WIDTH // SSM_GROUPS
    hpg = SSM_HEADS // SSM_GROUPS
    y_parts = []
    for g in range(SSM_GROUPS):
        bg = bc_act[:, g * SSM_STATE:(g + 1) * SSM_STATE].astype(BF16)
        cg = bc_act[:, (SSM_GROUPS + g) * SSM_STATE:(SSM_GROUPS + g + 1) * SSM_STATE].astype(BF16)
        cb = _dot_nt(cg, bg)
        s_old = s_ref[g * gw:(g + 1) * gw, :]
        y_inter = _dot_nt(cg, s_old.astype(BF16))
        pairs = []
        for j in range(hpg // 2):
            x_pair = xs_bf[:, g * gw + j * LANES:g * gw + (j + 1) * LANES]
            res = []
            for hh_ in range(2):
                h = g * hpg + 2 * j + hh_
                dec = jnp.exp(jnp.where(causal, cs[:, h:h + 1] - cs_t[h:h + 1, :], -jnp.inf))
                mm = (cb * dec * dt_t[h:h + 1, :]).astype(BF16)
                res.append(_dot(mm, x_pair))
            pairs.append(jnp.where(half, res[0], res[1]))
        y_intra = jnp.concatenate(pairs, axis=1)
        y_parts.append(y_intra + e_b[:, g * gw:(g + 1) * gw] * y_inter)
        upd = _dot_tn(xw_bf[:, g * gw:(g + 1) * gw], bg)
        for hh_ in range(hpg):
            h = g * hpg + hh_
            lo_r = g * gw + hh_ * SSM_HEADDIM
            s_ref[lo_r:lo_r + SSM_HEADDIM, :] = (
                ecs[CHUNK - 1:CHUNK, h:h + 1] * s_ref[lo_r:lo_r + SSM_HEADDIM, :]
                + upd[hh_ * SSM_HEADDIM:(hh_ + 1) * SSM_HEADDIM, :])
    y = jnp.concatenate(y_parts, axis=1)
    y_s = (y + dskip_ref[...] * xs_act) * proj_ref[:, SEG_Z:SEG_Z + SSM_WIDTH]
    y_b = jnp.concatenate(
        [_rms(y_s[:, g * gw:(g + 1) * gw]) for g in range(SSM_GROUPS)], axis=1) * snorm_ref[...]

    merged_ref[...] = (proj_ref[:, SEG_GATES:SEG_GATES + D_MODEL] * y_a
                       + proj_ref[:, SEG_GATES + D_MODEL:SEG_GATES + 2 * D_MODEL] * y_b
                       ).astype(merged_ref.dtype)


def _const_spec(shape, single_buffer=False):
    kwargs = dict(pipeline_mode=pl.Buffered(1)) if single_buffer else {}
    return pl.BlockSpec(shape, lambda *_: (0,) * len(shape), **kwargs)


def _mixer_params(w):
    return [w["bias_row"], w["alog_row"], w["dskip_row"], w["ml_head_norm"].reshape(1, D_MODEL),
            w["ssm_norm"].reshape(1, SSM_WIDTH), w["e2"]]


_MIXER_PARAM_SHAPES = [(1, LANES), (1, LANES), (1, SSM_WIDTH), (1, D_MODEL), (1, SSM_WIDTH),
                       (2 * LANES, SSM_WIDTH)]


def _mixer_chunk(proj, w, *, n_seq, n_chunks, seq_per_step):
    def per_seq(shape):
        return pl.BlockSpec((seq_per_step,) + shape, lambda b, c: (b,) + (0,) * len(shape))

    state_shapes = [(ML_HEADS, ML_DK, ML_DV), (ML_HEADS, ML_DK), (1, LANES), (SSM_WIDTH, SSM_STATE)]
    in_specs = [pl.BlockSpec((seq_per_step, CHUNK, PROJ_DIM), lambda b, c: (b, c, 0))] + [
        _const_spec(s) for s in _MIXER_PARAM_SHAPES]
    out_specs = [pl.BlockSpec((seq_per_step, CHUNK, D_MODEL), lambda b, c: (b, c, 0))] + [
        per_seq(s) for s in state_shapes]
    out_shape = [jax.ShapeDtypeStruct((n_seq, n_chunks * CHUNK, D_MODEL), BF16)] + [
        jax.ShapeDtypeStruct((n_seq,) + s, F32) for s in state_shapes]
    merged, *states = pl.pallas_call(
        _mixer_kernel,
        grid=(n_seq // seq_per_step, n_chunks),
        in_specs=in_specs,
        out_specs=out_specs,
        out_shape=out_shape,
        compiler_params=_params(),
        name="mixer_chunk",
    )(proj.reshape(n_seq, n_chunks * CHUNK, PROJ_DIM), *_mixer_params(w))
    return (merged.reshape(n_seq * n_chunks * CHUNK, D_MODEL), *states)


def _step_kernel(proj_ref, conv0_ref, n0_ref, m0_ref, c0_ref, s0_ref,
                 bias_ref, alog_ref, dskip_ref, hnorm_ref, snorm_ref, e2_ref, convw_ref, convb_ref,
                 merged_ref, conv_ref, n_ref, m_ref, c_ref, s_ref,
                 qf_ref, kt_ref, vb_ref, cmf_ref, bb_ref, xwt_ref, dec_ref, accq_ref, accs_ref):
    b = pl.program_id(0)
    ns = proj_ref.shape[0]
    gw = SSM_WIDTH // SSM_GROUPS
    hpg = SSM_HEADS // SSM_GROUPS
    lane = lax.broadcasted_iota(jnp.int32, (ns, LANES), 1)
    lane1 = lax.broadcasted_iota(jnp.int32, (1, LANES), 1)
    is_dt = lane < SM_IG
    is_li = (lane >= SM_IG) & (lane < SM_FG)
    is_lf = (lane >= SM_FG) & (lane < SM_FG + ML_HEADS)

    def gate_scalars():
        pre = proj_ref[:, SEG_SMALL:SEG_SMALL + LANES] + bias_ref[...]
        dt = jnp.where(is_dt, _softplus(pre), 0.0)
        lf = pltpu.roll(jnp.where(is_lf, -_softplus(-pre), 0.0), LANES - ML_HEADS, axis=1)
        li = jnp.where(is_li, pre, 0.0)
        inter = lf + m0_ref[...]
        m_t = jnp.maximum(inter, li)
        a_row = jnp.where(lane1 < SM_IG, -jnp.exp(alog_ref[...]), 0.0)
        return dict(dt=dt, m_t=m_t, w_intra=jnp.exp(li - m_t), w_inter=jnp.exp(inter - m_t),
                    e=jnp.exp(dt * a_row))

    def conv_act(lo, hi, seg, store):
        x_new = proj_ref[:, seg:seg + hi - lo]
        st = [conv0_ref[:, j * XBC_DIM + lo:j * XBC_DIM + hi] for j in range(HIST)]
        acc = st[0] * convw_ref[0:1, lo:hi]
        for j in range(1, HIST):
            acc = acc + st[j] * convw_ref[j:j + 1, lo:hi]
        acc = acc + x_new * convw_ref[HIST:HIST + 1, lo:hi]
        if store:
            for j in range(1, HIST):
                conv_ref[j - 1, :, lo:hi] = st[j]
            conv_ref[HIST - 1, :, lo:hi] = x_new
        return _silu(acc + convb_ref[:, lo:hi])

    def head_col(x, h):
        return x[:, SM_IG + h:SM_IG + h + 1]

    @pl.when(b == 0)
    def _():
        gs = gate_scalars()
        m_ref[...] = gs["m_t"]
        dec_ref[...] = jnp.where(is_dt, gs["e"], gs["w_inter"])
        qf_ref[...] = (proj_ref[:, SEG_Q:SEG_Q + ML_QK] * (ML_DK ** -0.5)).astype(BF16).astype(F32)
        vb_ref[...] = proj_ref[:, SEG_V:SEG_V + D_MODEL].astype(BF16)
        for h in range(ML_HEADS):
            sl = slice(h * ML_DK, (h + 1) * ML_DK)
            kw = proj_ref[:, SEG_K + h * ML_DK:SEG_K + (h + 1) * ML_DK] * head_col(gs["w_intra"], h)
            n_ref[:, sl] = head_col(gs["w_inter"], h) * n0_ref[:, sl] + kw
            kt_ref[sl, :] = kw.T
        xs_act = conv_act(0, SSM_WIDTH, SEG_XS, True)
        bc_act = conv_act(SSM_WIDTH, XBC_DIM, SEG_BC, True)
        bb_ref[...] = bc_act[:, 0:SSM_GROUPS * SSM_STATE].astype(BF16)
        cmf_ref[...] = bc_act[:, SSM_GROUPS * SSM_STATE:BC_DIM]
        xw = xs_act * _expand_heads(gs["dt"], e2_ref[...])
        for g in range(SSM_GROUPS):
            xwt_ref[g * gw:(g + 1) * gw, :] = xw[:, g * gw:(g + 1) * gw].T
        accq_ref[...] = jnp.zeros(accq_ref.shape, F32)
        accs_ref[...] = jnp.zeros(accs_ref.shape, F32)

    for sb in range(c0_ref.shape[0]):
        seq = b * c0_ref.shape[0] + sb
        base = pl.multiple_of((seq // SUBLANES) * SUBLANES, SUBLANES)
        is_row = lax.broadcasted_iota(jnp.int32, (SUBLANES, 1), 0) == seq % SUBLANES
        is_seq = lax.broadcasted_iota(jnp.int32, (1, ns), 1) == seq
        q8 = jnp.where(is_row, qf_ref[pl.ds(base, SUBLANES), :], 0.0).astype(BF16)
        c8 = jnp.where(is_row, cmf_ref[pl.ds(base, SUBLANES), :], 0.0).astype(BF16)
        drow = dec_ref[pl.ds(seq, 1), :]
        for h in range(ML_HEADS):
            c_old = c0_ref[sb, h]
            accq_ref[pl.ds(base, SUBLANES), h * ML_DV:(h + 1) * ML_DV] += _dot(
                q8[:, h * ML_DK:(h + 1) * ML_DK], c_old.astype(BF16))
            kt = jnp.where(is_seq, kt_ref[h * ML_DK:(h + 1) * ML_DK, :], 0.0).astype(BF16)
            c_ref[sb, h] = (drow[:, SM_IG + h:SM_IG + h + 1] * c_old
                            + _dot(kt, vb_ref[:, h * ML_DV:(h + 1) * ML_DV]))
        for g in range(SSM_GROUPS):
            s_old = s0_ref[sb, g * gw:(g + 1) * gw, :]
            accs_ref[pl.ds(base, SUBLANES), g * gw:(g + 1) * gw] += _dot_nt(
                c8[:, g * SSM_STATE:(g + 1) * SSM_STATE], s_old.astype(BF16))
            xwt = jnp.where(is_seq, xwt_ref[g * gw:(g + 1) * gw, :], 0.0).astype(BF16)
            upd = _dot(xwt, bb_ref[:, g * SSM_STATE:(g + 1) * SSM_STATE])
            for hh_ in range(hpg):
                h = g * hpg + hh_
                rs = slice(hh_ * SSM_HEADDIM, (hh_ + 1) * SSM_HEADDIM)
                s_ref[sb, g * gw + hh_ * SSM_HEADDIM:g * gw + (hh_ + 1) * SSM_HEADDIM, :] = (
                    drow[:, h:h + 1] * s_old[rs, :] + upd[rs, :])

    @pl.when(b == pl.num_programs(0) - 1)
    def _():
        gs = gate_scalars()
        q = qf_ref[...]
        y_a_parts = []
        for h in range(ML_HEADS):
            qh = q[:, h * ML_DK:(h + 1) * ML_DK]
            kh = proj_ref[:, SEG_K + h * ML_DK:SEG_K + (h + 1) * ML_DK].astype(BF16).astype(F32)
            vh = vb_ref[:, h * ML_DV:(h + 1) * ML_DV].astype(F32)
            w_inter = head_col(gs["w_inter"], h)
            s = jnp.sum(qh * kh, axis=1, keepdims=True) * head_col(gs["w_intra"], h)
            qn = jnp.sum(qh * n0_ref[:, h * ML_DK:(h + 1) * ML_DK], axis=1, keepdims=True)
            num = s.astype(BF16).astype(F32) * vh + w_inter * accq_ref[:, h * ML_DV:(h + 1) * ML_DV]
            den = s + w_inter * qn
            hh = num / jnp.maximum(jnp.abs(den), jnp.exp(-head_col(gs["m_t"], h)))
            hn = _rms(hh) * hnorm_ref[:, h * ML_DV:(h + 1) * ML_DV]
            y_a_parts.append(proj_ref[:, SEG_OG + h * ML_DV:SEG_OG + (h + 1) * ML_DV] * hn)
        y_a = jnp.concatenate(y_a_parts, axis=1)

        xs_act = conv_act(0, SSM_WIDTH, SEG_XS, False)
        bb = bb_ref[...].astype(F32)
        cm = cmf_ref[...].astype(BF16).astype(F32)
        e2 = e2_ref[...]
        x_dt = xs_act * _expand_heads(gs["dt"], e2)
        e_e = _expand_heads(jnp.where(is_dt, gs["e"], 0.0), e2)
        y_parts = []
        for g in range(SSM_GROUPS):
            gl = slice(g * SSM_STATE, (g + 1) * SSM_STATE)
            cb = jnp.sum(cm[:, gl] * bb[:, gl], axis=1, keepdims=True)
            cl = slice(g * gw, (g + 1) * gw)
            y_parts.append(cb * x_dt[:, cl] + e_e[:, cl] * accs_ref[:, cl])
        y = jnp.concatenate(y_parts, axis=1)
        y_s = (y + dskip_ref[...] * xs_act) * proj_ref[:, SEG_Z:SEG_Z + SSM_WIDTH]
        y_b = jnp.concatenate(
            [_rms(y_s[:, g * gw:(g + 1) * gw]) for g in range(SSM_GROUPS)], axis=1) * snorm_ref[...]
        merged_ref[...] = (proj_ref[:, SEG_GATES:SEG_GATES + D_MODEL] * y_a
                           + proj_ref[:, SEG_GATES + D_MODEL:SEG_GATES + 2 * D_MODEL] * y_b)


def _mixer_step(proj, conv0, n0, m0, c0, s0, w, *, seq_per_step):
    ns = proj.shape[0]

    def per_seq(shape):
        return pl.BlockSpec((seq_per_step,) + shape, lambda b: (b,) + (0,) * len(shape))

    row_shapes = [(ns, HIST * XBC_DIM), (ns, ML_QK), (ns, LANES)]
    mat_shapes = [(ML_HEADS, ML_DK, ML_DV), (SSM_WIDTH, SSM_STATE)]
    conv_shapes = [(CONV_W, XBC_DIM), (1, XBC_DIM)]
    in_specs = ([_const_spec((ns, PROJ_DIM), True)] + [_const_spec(s, True) for s in row_shapes]
                + [per_seq(s) for s in mat_shapes]
                + [_const_spec(s) for s in _MIXER_PARAM_SHAPES + conv_shapes])
    out_rows = [(HIST, ns, XBC_DIM)] + row_shapes[1:]
    out_specs = ([_const_spec((ns, D_MODEL))] + [_const_spec(s) for s in out_rows]
                 + [per_seq(s) for s in mat_shapes])
    out_shape = ([jax.ShapeDtypeStruct((ns, D_MODEL), F32)]
                 + [jax.ShapeDtypeStruct(s, F32) for s in out_rows]
                 + [jax.ShapeDtypeStruct((ns,) + s, F32) for s in mat_shapes])
    scratch = [
        pltpu.VMEM((ns, ML_QK), F32),
        pltpu.VMEM((ML_QK, ns), F32),
        pltpu.VMEM((ns, D_MODEL), BF16),
        pltpu.VMEM((ns, SSM_GROUPS * SSM_STATE), F32),
        pltpu.VMEM((ns, SSM_GROUPS * SSM_STATE), BF16),
        pltpu.VMEM((SSM_WIDTH, ns), F32),
        pltpu.VMEM((ns, LANES), F32),
        pltpu.VMEM((ns, D_MODEL), F32),
        pltpu.VMEM((ns, SSM_WIDTH), F32),
    ]
    return pl.pallas_call(
        _step_kernel,
        grid=(ns // seq_per_step,),
        in_specs=in_specs,
        out_specs=out_specs,
        out_shape=out_shape,
        scratch_shapes=scratch,
        compiler_params=_params(1),
        name="mixer_step",
    )(proj, conv0, n0, m0, c0, s0, *_mixer_params(w), w["ssm_conv_w"], w["ssm_conv_b"])


def _pad_lanes(vec, offset):
    pad = [(0, 0)] * (vec.ndim - 1) + [(offset, LANES - offset - vec.shape[-1])]
    return jnp.pad(vec.astype(F32), pad)


def kernel(x_prompt, x_sample, state_conv, state_mlstm_C, state_mlstm_n, state_mlstm_m, state_ssm, ffn1_norm, ffn1_w_gate, ffn1_w_up, ffn1_w_down, mix_norm, w_in, ml_i_bias, ml_f_bias, ml_head_norm, ssm_conv_w, ssm_conv_b, ssm_dt_bias, ssm_A_log, ssm_D, ssm_norm, w_out, ffn2_norm, ffn2_w_gate, ffn2_w_up, ffn2_w_down, final_norm):
    depth = w_in.shape[0]
    assert depth == 1, "the final norm is fused into the (single) layer's second FFN"
    bp, seq, d = x_prompt.shape
    bs = x_sample.shape[0]
    l = 0
    tm = ROW_TILE
    assert x_sample.shape[1] == 1 and seq % CHUNK == 0 and seq % tm == 0
    assert bp % SEQ_PER_STEP == 0 and bs % SUBLANES == 0 and SUBLANES % SEQ_PER_STEP == 0

    head_of_lane = jnp.arange(SSM_WIDTH) // SSM_HEADDIM
    w = {
        "bias_row": (_pad_lanes(ssm_dt_bias[l], SM_DT) + _pad_lanes(ml_i_bias[l], SM_IG)
                     + _pad_lanes(ml_f_bias[l], SM_FG)).reshape(1, LANES),
        "alog_row": _pad_lanes(ssm_A_log[l], SM_DT).reshape(1, LANES),
        "ml_head_norm": ml_head_norm[l], "ssm_conv_w": ssm_conv_w[l],
        "ssm_conv_b": ssm_conv_b[l].reshape(1, XBC_DIM),
        "dskip_row": jnp.repeat(ssm_D[l].astype(F32), SSM_HEADDIM).reshape(1, SSM_WIDTH),
        "ssm_norm": ssm_norm[l],
        "e2": ((jnp.arange(2 * LANES) % LANES)[:, None] == head_of_lane[None, :]).astype(BF16),
    }

    xp = x_prompt.reshape(bp * seq, d)
    xs = x_sample.reshape(bs, d)
    xp1, xs1 = _ffn(xp, xs, ffn1_norm[l], ffn1_w_gate[l], ffn1_w_up[l], ffn1_w_down[l], final_norm,
                    final_norm=False, tm=tm, tf=FFN_TILE)
    w_in_r = _regroup_w_in(jnp.swapaxes(w_in[l], 0, 1), tc=REGROUP_COLS)
    proj_p, proj_s, p_tail = _inproj(xp1, xs1, mix_norm[l], w_in_r, _activation_rows(), w["ssm_conv_w"],
                                     w["ssm_conv_b"], tm=tm, tn=PROJ_TILE, rows_per_seq=seq)
    p_conv = p_tail[seq // tm - 1::seq // tm, SUBLANES - HIST:, :]

    mp, p_c, p_n, p_m, p_s = _mixer_chunk(proj_p, w, n_seq=bp, n_chunks=seq // CHUNK,
                                          seq_per_step=SEQ_PER_STEP)
    ms, s_conv, s_n, s_m, s_c, s_s = _mixer_step(
        proj_s, state_conv[l].reshape(bs, HIST * XBC_DIM), state_mlstm_n[l].reshape(bs, ML_QK),
        _pad_lanes(state_mlstm_m[l], SM_IG), state_mlstm_C[l],
        state_ssm[l].reshape(bs, SSM_WIDTH, SSM_STATE), w, seq_per_step=SEQ_PER_STEP)

    xp2, xs2 = _outproj(mp, ms, w_out[l], xp1, xs1, tm=OUT_ROW_TILE, tn=OUT_COL_TILE)
    yp, ys = _ffn(xp2, xs2, ffn2_norm[l], ffn2_w_gate[l], ffn2_w_up[l], ffn2_w_down[l], final_norm,
                  final_norm=True, tm=tm, tf=FFN_TILE)

    ssm_shape = (SSM_HEADS, SSM_HEADDIM, SSM_STATE)
    return (yp.reshape(bp, seq, d), ys.reshape(bs, 1, d),
            p_conv[None], p_c[None], p_n[None], p_m[:, 0, :ML_HEADS][None], p_s.reshape(bp, *ssm_shape)[None],
            jnp.swapaxes(s_conv, 0, 1)[None], s_c[None], s_n.reshape(bs, ML_HEADS, ML_DK)[None],
            s_m[:, SM_IG:SM_IG + ML_HEADS][None], s_s.reshape(bs, *ssm_shape)[None])
```
